```python
import math
import jax, jax.numpy as jnp
from jax import lax
import numpy as np

D_MODEL = 1024
BATCH = 2
SEQ = 16384
DEPTH = 1
DEC_BATCH = 32
DEC_SEQ = 16
PAST_LEN = 4096

CHUNK = 64
A_HEADS = 8
A_HEAD_DIM = 64
A_WIDTH = A_HEADS * A_HEAD_DIM
A_PREV_CHUNKS = 8
A_WINDOW = (A_PREV_CHUNKS + 1) * CHUNK
A_REL_MAX = 2 * CHUNK
B_HEADS = 4
B_HEAD_DIM = 64
B_VDIM = 2 * B_HEAD_DIM
B_WIDTH = B_HEADS * B_VDIM
M_TOKENS = 256
M_HEADS = 4
M_HEAD_DIM = D_MODEL // M_HEADS
D_FF = 2816
N_BRANCH = 2
IN_COLS = 3 * A_WIDTH + 3 * B_WIDTH
IN_SPLITS = [A_WIDTH, 2 * A_WIDTH, 3 * A_WIDTH, 3 * A_WIDTH + B_WIDTH, 3 * A_WIDTH + 2 * B_WIDTH]
Q_BLOCK = 128
N_NORMS = 9
NORM_EPS = 1e-6
NEG_INF = -1e30

kernel_name = 'hybrid_stream_encoder_step'


def rmsnorm(x, g):
    x32 = x.astype(jnp.float32)
    y = x32 * lax.rsqrt(jnp.mean(x32 * x32, axis=-1, keepdims=True) + NORM_EPS)
    return (y * g.astype(jnp.float32)).astype(x.dtype)


def swiglu(x, w_up, w_down):
    gate, up = jnp.split(x @ w_up, 2, axis=-1)
    return (jax.nn.silu(gate) * up) @ w_down


def alibi_slopes():
    return jnp.asarray(2.0 ** (-8.0 * np.arange(1, B_HEADS + 1) / B_HEADS), dtype=jnp.float32)


def band_attend(q, k, v, pos_q, pos_k, rel_bias):
    cq = pos_q[:, None] // CHUNK
    ck = pos_k[None, :] // CHUNK
    ok = (pos_k[None, :] >= 0) & (ck <= cq) & (ck >= cq - A_PREV_CHUNKS)
    rel = jnp.clip(pos_q[:, None] - pos_k[None, :], -A_REL_MAX, A_REL_MAX) + A_REL_MAX
    bias = rel_bias.astype(jnp.float32)[:, rel]
    s = jnp.einsum('nqhd,nkhd->nhqk', q, k).astype(jnp.float32) * (A_HEAD_DIM ** -0.5) + bias
    p = jax.nn.softmax(jnp.where(ok, s, NEG_INF), axis=-1).astype(v.dtype)
    return jnp.einsum('nhqk,nkhd->nqhd', p, v)


def band_attn_prompt(q, k, v, rel_bias):
    n, s_len = q.shape[:2]
    nc = s_len // CHUNK
    pad = A_PREV_CHUNKS * CHUNK
    kp = jnp.pad(k, ((0, 0), (pad, 0), (0, 0), (0, 0)))
    vp = jnp.pad(v, ((0, 0), (pad, 0), (0, 0), (0, 0)))
    qc = q.reshape(n, nc, CHUNK, A_HEADS, A_HEAD_DIM).swapaxes(0, 1)

    def one_chunk(args):
        c, qi = args
        kb = lax.dynamic_slice_in_dim(kp, c * CHUNK, A_WINDOW, axis=1)
        vb = lax.dynamic_slice_in_dim(vp, c * CHUNK, A_WINDOW, axis=1)
        pos_q = c * CHUNK + jnp.arange(CHUNK, dtype=jnp.int32)
        pos_k = (c - A_PREV_CHUNKS) * CHUNK + jnp.arange(A_WINDOW, dtype=jnp.int32)
        return band_attend(qi, kb, vb, pos_q, pos_k, rel_bias)

    o = lax.map(one_chunk, (jnp.arange(nc, dtype=jnp.int32), qc))
    return o.swapaxes(0, 1).reshape(n, s_len, A_HEADS, A_HEAD_DIM)


def diff_attend(q, k, v, pos_q, pos_k, lam, subln_g, lam_init):
    ok = (pos_k[None, :] // CHUNK) <= (pos_q[:, None] // CHUNK)
    dist = jnp.abs(pos_q[:, None] - pos_k[None, :]).astype(jnp.float32)
    bias = -alibi_slopes()[:, None, None] * dist
    s = jnp.einsum('nqhcd,nkhcd->nhcqk', q, k).astype(jnp.float32) * (B_HEAD_DIM ** -0.5) + bias[:, None]
    p = jax.nn.softmax(jnp.where(ok, s, NEG_INF), axis=-1)
    w = (p[:, :, 0] - lam * p[:, :, 1]).astype(v.dtype)
    o = jnp.einsum('nhqk,nkhe->nqhe', w, v)
    return rmsnorm(o, subln_g) * (1.0 - lam_init)


def diff_attn_prompt(q, k, v, lam, subln_g, lam_init):
    n, s_len = q.shape[:2]
    nb = s_len // Q_BLOCK
    pos_k = jnp.arange(s_len, dtype=jnp.int32)
    qb = q.reshape(n, nb, Q_BLOCK, B_HEADS, 2, B_HEAD_DIM).swapaxes(0, 1)

    def one_block(args):
        i, qi = args
        pos_q = i * Q_BLOCK + jnp.arange(Q_BLOCK, dtype=jnp.int32)
        return diff_attend(qi, k, v, pos_q, pos_k, lam, subln_g, lam_init)

    o = lax.map(one_block, (jnp.arange(nb, dtype=jnp.int32), qb))
    return o.swapaxes(0, 1).reshape(n, s_len, B_HEADS, B_VDIM)


def mem_kv(mem, g, w_mkv):
    n, m, _ = mem.shape
    k, v = jnp.split(rmsnorm(mem, g) @ w_mkv, 2, axis=-1)
    return k.reshape(n, m, M_HEADS, M_HEAD_DIM), v.reshape(n, m, M_HEADS, M_HEAD_DIM)


def mem_attend(xn, mk, mv, w_mq, w_mo):
    n, t, _ = xn.shape
    q = (xn @ w_mq).reshape(n, t, M_HEADS, M_HEAD_DIM)
    s = jnp.einsum('nqhd,nkhd->nhqk', q, mk).astype(jnp.float32) * (M_HEAD_DIM ** -0.5)
    p = jax.nn.softmax(s, axis=-1).astype(mv.dtype)
    o = jnp.einsum('nhqk,nkhd->nqhd', p, mv).reshape(n, t, D_MODEL)
    return o @ w_mo


def pre_mixer(x, g, up1, down1, w_in, w_gate, b_gate):
    h = x + 0.5 * rmsnorm(swiglu(rmsnorm(x, g[0]), up1, down1), g[1])
    u = rmsnorm(h, g[2])
    n, t, _ = u.shape
    qa, ka, va, qb, kb, vb = jnp.split(u @ w_in, IN_SPLITS, axis=-1)
    qa = qa.reshape(n, t, A_HEADS, A_HEAD_DIM)
    ka = ka.reshape(n, t, A_HEADS, A_HEAD_DIM)
    va = va.reshape(n, t, A_HEADS, A_HEAD_DIM)
    qb = qb.reshape(n, t, B_HEADS, 2, B_HEAD_DIM)
    kb = kb.reshape(n, t, B_HEADS, 2, B_HEAD_DIM)
    vb = vb.reshape(n, t, B_HEADS, B_VDIM)
    gates = jax.nn.sigmoid((u @ w_gate + b_gate).astype(jnp.float32)).astype(u.dtype)
    gates = gates.reshape(n, t, N_BRANCH, D_MODEL)
    return h, gates, qa, ka, va, qb, kb, vb


def post_mixer(h, ya, yb, gates, mk, mv, g, w_br_a, w_br_b, w_out, w_mq, w_mo, up2, down2):
    n, t, _ = h.shape
    merged = (gates[:, :, 0] * (ya.reshape(n, t, A_WIDTH) @ w_br_a)
              + gates[:, :, 1] * (yb.reshape(n, t, B_WIDTH) @ w_br_b))
    h = h + rmsnorm(merged @ w_out, g[3])
    h = h + rmsnorm(mem_attend(rmsnorm(h, g[4]), mk, mv, w_mq, w_mo), g[5])
    return h + 0.5 * rmsnorm(swiglu(rmsnorm(h, g[7]), up2, down2), g[8])


def _normal(k, shape, scale):
    return jax.random.normal(k, shape, jnp.float32) * scale


def setup_inputs(seed: int = 0) -> dict:
    key = jax.random.key(seed)
    ks = jax.random.split(key, 32)
    la = min(A_WINDOW, PAST_LEN)
    d_in = D_MODEL ** -0.5
    return {
        'x_prompt': _normal(ks[0], (BATCH, SEQ, D_MODEL), 1.0),
        'x_sample': _normal(ks[1], (DEC_BATCH, DEC_SEQ, D_MODEL), 1.0),
        'cache_a_k': _normal(ks[2], (DEPTH, DEC_BATCH, la, A_HEADS, A_HEAD_DIM), 1.0),
        'cache_a_v': _normal(ks[3], (DEPTH, DEC_BATCH, la, A_HEADS, A_HEAD_DIM), 1.0),
        'cache_b_k': _normal(ks[4], (DEPTH, DEC_BATCH, PAST_LEN, B_HEADS, 2, B_HEAD_DIM), 1.0),
        'cache_b_v': _normal(ks[5], (DEPTH, DEC_BATCH, PAST_LEN, B_HEADS, B_VDIM), 1.0),
        'cache_mem_k': _normal(ks[6], (DEPTH, DEC_BATCH, M_TOKENS, M_HEADS, M_HEAD_DIM), 1.0),
        'cache_mem_v': _normal(ks[7], (DEPTH, DEC_BATCH, M_TOKENS, M_HEADS, M_HEAD_DIM), 1.0),
        'mem_prompt': _normal(ks[8], (BATCH, M_TOKENS, D_MODEL), 1.0),
        'w_in': _normal(ks[9], (DEPTH, D_MODEL, IN_COLS), d_in),
        'w_gate': _normal(ks[10], (DEPTH, D_MODEL, N_BRANCH * D_MODEL), d_in),
        'b_gate': _normal(ks[11], (DEPTH, N_BRANCH * D_MODEL), 0.1),
        'rel_bias': _normal(ks[12], (DEPTH, A_HEADS, 2 * A_REL_MAX + 1), 0.5),
        'lam_qk': _normal(ks[13], (DEPTH, 4, B_HEAD_DIM), 0.1),
        'subln_g': 1.0 + _normal(ks[14], (DEPTH, B_VDIM), 0.05),
        'w_br_a': _normal(ks[15], (DEPTH, A_WIDTH, D_MODEL), A_WIDTH ** -0.5),
        'w_br_b': _normal(ks[16], (DEPTH, B_WIDTH, D_MODEL), B_WIDTH ** -0.5),
        'w_out': _normal(ks[17], (DEPTH, D_MODEL, D_MODEL), d_in),
        'w_mq': _normal(ks[18], (DEPTH, D_MODEL, D_MODEL), d_in),
        'w_mkv': _normal(ks[19], (DEPTH, D_MODEL, 2 * D_MODEL), d_in),
        'w_mo': _normal(ks[20], (DEPTH, D_MODEL, D_MODEL), d_in),
        'norm_g': 1.0 + _normal(ks[21], (DEPTH, N_NORMS, D_MODEL), 0.05),
        'ffn1_up': _normal(ks[22], (DEPTH, D_MODEL, 2 * D_FF), d_in),
        'ffn1_down': _normal(ks[23], (DEPTH, D_FF, D_MODEL), D_FF ** -0.5),
        'ffn2_up': _normal(ks[24], (DEPTH, D_MODEL, 2 * D_FF), d_in),
        'ffn2_down': _normal(ks[25], (DEPTH, D_FF, D_MODEL), D_FF ** -0.5),
    }


def reference(x_prompt, x_sample, cache_a_k, cache_a_v, cache_b_k, cache_b_v, cache_mem_k, cache_mem_v,
              mem_prompt, w_in, w_gate, b_gate, rel_bias, lam_qk, subln_g, w_br_a, w_br_b, w_out,
              w_mq, w_mkv, w_mo, norm_g, ffn1_up, ffn1_down, ffn2_up, ffn2_down):
    s_len = x_prompt.shape[1]
    t_len = x_sample.shape[1]
    past = cache_b_k.shape[2]
    l_a = cache_a_k.shape[2]
    keep_a = min(A_WINDOW, s_len)
    pos_q_s = past + jnp.arange(t_len, dtype=jnp.int32)
    pos_k_sa = jnp.concatenate([past - l_a + jnp.arange(l_a, dtype=jnp.int32), pos_q_s])
    pos_k_sb = jnp.arange(past + t_len, dtype=jnp.int32)

    xp, xs = x_prompt, x_sample
    ak_p, av_p, bk_p, bv_p, mk_p, mv_p = [], [], [], [], [], []
    ak_s, av_s, bk_s, bv_s = [], [], [], []
    for l in range(DEPTH):
        g = norm_g[l]
        lam_init = 0.8 - 0.6 * math.exp(-0.3 * l)
        lq = lam_qk[l].astype(jnp.float32)
        lam = jnp.exp(jnp.sum(lq[0] * lq[1])) - jnp.exp(jnp.sum(lq[2] * lq[3])) + lam_init

        h, gates, qa, ka, va, qb, kb, vb = pre_mixer(xp, g, ffn1_up[l], ffn1_down[l], w_in[l], w_gate[l], b_gate[l])
        ya = band_attn_prompt(qa, ka, va, rel_bias[l])
        yb = diff_attn_prompt(qb, kb, vb, lam, subln_g[l], lam_init)
        mk, mv = mem_kv(mem_prompt, g[6], w_mkv[l])
        xp = post_mixer(h, ya, yb, gates, mk, mv, g, w_br_a[l], w_br_b[l], w_out[l], w_mq[l], w_mo[l],
                        ffn2_up[l], ffn2_down[l])
        ak_p.append(ka[:, s_len - keep_a:])
        av_p.append(va[:, s_len - keep_a:])
        bk_p.append(kb)
        bv_p.append(vb)
        mk_p.append(mk)
        mv_p.append(mv)

        h, gates, qa, ka, va, qb, kb, vb = pre_mixer(xs, g, ffn1_up[l], ffn1_down[l], w_in[l], w_gate[l], b_gate[l])
        ka_all = jnp.concatenate([cache_a_k[l], ka], axis=1)
        va_all = jnp.concatenate([cache_a_v[l], va], axis=1)
        ya = band_attend(qa, ka_all, va_all, pos_q_s, pos_k_sa, rel_bias[l])
        kb_all = jnp.concatenate([cache_b_k[l], kb], axis=1)
        vb_all = jnp.concatenate([cache_b_v[l], vb], axis=1)
        yb = diff_attend(qb, kb_all, vb_all, pos_q_s, pos_k_sb, lam, subln_g[l], lam_init)
        xs = post_mixer(h, ya, yb, gates, cache_mem_k[l], cache_mem_v[l], g, w_br_a[l], w_br_b[l], w_out[l],
                        w_mq[l], w_mo[l], ffn2_up[l], ffn2_down[l])
        ak_s.append(ka_all[:, ka_all.shape[1] - l_a:])
        av_s.append(va_all[:, va_all.shape[1] - l_a:])
        bk_s.append(kb)
        bv_s.append(vb)

    return (xp, xs,
            jnp.stack(ak_p), jnp.stack(av_p), jnp.stack(bk_p), jnp.stack(bv_p), jnp.stack(mk_p), jnp.stack(mv_p),
            jnp.stack(ak_s), jnp.stack(av_s), jnp.stack(bk_s), jnp.stack(bv_s))
```

```python
import functools
import math

import jax
import jax.numpy as jnp
import numpy as np
from jax import lax
from jax.experimental import pallas as pl
from jax.experimental.pallas import tpu as pltpu

F32 = jnp.float32
BF16 = jnp.bfloat16

CHUNK = 64
A_HEADS = 8
A_HEAD_DIM = 64
A_WIDTH = A_HEADS * A_HEAD_DIM
A_PREV_CHUNKS = 8
A_WINDOW = (A_PREV_CHUNKS + 1) * CHUNK
A_REL_MAX = 2 * CHUNK
B_HEADS = 4
B_HEAD_DIM = 64
B_VDIM = 2 * B_HEAD_DIM
B_WIDTH = B_HEADS * B_VDIM
M_HEADS = 4
NORM_EPS = 1e-6
NEG_INF = -1e30

V7X_VMEM_BYTES = 64 * 1024 * 1024
VMEM_LIMIT = V7X_VMEM_BYTES - 8 * 1024 * 1024

ROW_TILE = 512
A_QTILE = 256
B_TILE = 256


def _params(*sem):
    return pltpu.CompilerParams(dimension_semantics=sem, vmem_limit_bytes=VMEM_LIMIT)


def _const_spec(shape):
    zeros = (0,) * len(shape)
    return pl.BlockSpec(shape, lambda *_: zeros, pipeline_mode=pl.Buffered(1))


def _row_tile(rows):
    return max(t for t in range(16, min(ROW_TILE, rows) + 1, 16) if rows % t == 0)


def _rms(x, g):
    return x * lax.rsqrt(jnp.mean(x * x, axis=-1, keepdims=True) + NORM_EPS) * g


def _dot(a, b):
    return jnp.dot(a, b, preferred_element_type=F32)


def _dot_nt(a, b):
    return lax.dot_general(a, b, (((1,), (1,)), ((), ())), preferred_element_type=F32)


def _ffn_kernel(x_ref, g_ref, up_ref, down_ref, o_ref, *, pre, post, d_ff):
    x = x_ref[...]
    xn = _rms(x, g_ref[pre:pre + 1, :]).astype(BF16)
    gu = _dot(xn, up_ref[...])
    gate = gu[:, :d_ff]
    act = (gate * jax.nn.sigmoid(gate) * gu[:, d_ff:]).astype(BF16)
    y = _dot(act, down_ref[...])
    o_ref[...] = x + 0.5 * _rms(y, g_ref[post:post + 1, :])


def _ffn(x, g, up, down, pre, post):
    rows, d = x.shape
    d_ff = down.shape[0]
    tm = _row_tile(rows)
    return pl.pallas_call(
        functools.partial(_ffn_kernel, pre=pre, post=post, d_ff=d_ff),
        grid=(rows // tm,),
        in_specs=[pl.BlockSpec((tm, d), lambda i: (i, 0)),
                  _const_spec(g.shape), _const_spec(up.shape), _const_spec(down.shape)],
        out_specs=pl.BlockSpec((tm, d), lambda i: (i, 0)),
        out_shape=jax.ShapeDtypeStruct((rows, d), F32),
        compiler_params=_params("parallel"),
        name="ffn",
    )(x, g, up, down)


def _qkv_kernel(h_ref, g_ref, w_ref, *out_refs, a_f32):
    u = _rms(h_ref[...], g_ref[2:3, :]).astype(BF16)
    y = _dot(u, w_ref[...])
    w = A_WIDTH
    qa, ka, va, qb, kb, vb = (y[:, j * w:(j + 1) * w] for j in range(6))
    scale = A_HEAD_DIM ** -0.5
    out_refs[0][...] = (qa * scale).astype(BF16)
    out_refs[1][...] = ka.astype(BF16)
    out_refs[2][...] = va.astype(BF16)
    out_refs[3][...] = (qb * scale).astype(BF16)
    out_refs[4][...] = kb.astype(BF16)
    out_refs[5][...] = vb.astype(BF16)
    out_refs[6][...] = kb
    out_refs[7][...] = vb
    if a_f32:
        out_refs[8][...] = ka
        out_refs[9][...] = va


def _qkv(h, g, w_in, a_f32):
    assert A_WIDTH == B_WIDTH and A_HEAD_DIM == B_HEAD_DIM
    rows, d = h.shape
    tm = _row_tile(rows)
    assert rows % tm == 0
    n_f32 = 4 if a_f32 else 2
    spec = pl.BlockSpec((tm, A_WIDTH), lambda i: (i, 0))
    return pl.pallas_call(
        functools.partial(_qkv_kernel, a_f32=a_f32),
        grid=(rows // tm,),
        in_specs=[pl.BlockSpec((tm, d), lambda i: (i, 0)), _const_spec(g.shape), _const_spec(w_in.shape)],
        out_specs=[spec] * (6 + n_f32),
        out_shape=[jax.ShapeDtypeStruct((rows, A_WIDTH), BF16)] * 6
        + [jax.ShapeDtypeStruct((rows, A_WIDTH), F32)] * n_f32,
        compiler_params=_params("parallel"),
        name="qkv",
    )(h, g, w_in)


def _memkv_kernel(m_ref, g_ref, w_ref, kv_ref, kvb_ref):
    y = _dot(_rms(m_ref[...], g_ref[6:7, :]).astype(BF16), w_ref[...])
    kv_ref[...] = y
    kvb_ref[...] = y.astype(BF16)


def _memkv(mem, g, w_mkv):
    rows, d = mem.shape
    n = w_mkv.shape[1]
    return pl.pallas_call(
        _memkv_kernel,
        grid=(1,),
        in_specs=[_const_spec(mem.shape), _const_spec(g.shape), _const_spec(w_mkv.shape)],
        out_specs=[pl.BlockSpec((rows, n), lambda i: (0, 0))] * 2,
        out_shape=[jax.ShapeDtypeStruct((rows, n), F32), jax.ShapeDtypeStruct((rows, n), BF16)],
        compiler_params=_params("arbitrary"),
        name="memkv",
    )(mem, g, w_mkv)


def _band_kernel(q_ref, k0_ref, k1_ref, k2_ref, v0_ref, v1_ref, v2_ref, bias_ref, o_ref):
    i = pl.program_id(1)
    q = q_ref[0]
    k = jnp.concatenate([k0_ref[0], k1_ref[0], k2_ref[0]], axis=0)
    v = jnp.concatenate([v0_ref[0], v1_ref[0], v2_ref[0]], axis=0)
    tq, tk = q.shape[0], k.shape[0]
    col = lax.broadcasted_iota(jnp.int32, (tq, tk), 1)
    valid = col + (i - 2) * tq >= 0
    outs = []
    for h in range(A_HEADS):
        sl = slice(h * A_HEAD_DIM, (h + 1) * A_HEAD_DIM)
        s = _dot_nt(q[:, sl], k[:, sl]) + bias_ref[h]
        s = jnp.where(valid, s, NEG_INF)
        e = jnp.exp(s - jnp.max(s, axis=-1, keepdims=True))
        l = jnp.sum(e, axis=-1, keepdims=True)
        outs.append(_dot(e.astype(BF16), v[:, sl]) / l)
    o_ref[0] = jnp.concatenate(outs, axis=1).astype(BF16)


def _band_prompt(qa, ka, va, bias):
    n, s_len, w = qa.shape
    t = A_QTILE
    assert s_len % t == 0 and 2 * t == A_PREV_CHUNKS * CHUNK
    qspec = pl.BlockSpec((1, t, w), lambda b, i: (b, i, 0))

    def kspec(back):
        return pl.BlockSpec((1, t, w), lambda b, i: (b, jnp.maximum(i - back, 0), 0))

    return pl.pallas_call(
        _band_kernel,
        grid=(n, s_len // t),
        in_specs=[qspec, kspec(2), kspec(1), kspec(0), kspec(2), kspec(1), kspec(0), _const_spec(bias.shape)],
        out_specs=qspec,
        out_shape=jax.ShapeDtypeStruct((n, s_len, w), BF16),
        compiler_params=_params("parallel", "parallel"),
        name="band_prompt",
    )(qa, ka, ka, ka, va, va, va, bias)


def _split_halves(q):
    lane = lax.broadcasted_iota(jnp.int32, q.shape, 1)
    zero = jnp.zeros_like(q)
    return jnp.concatenate([jnp.where(lane < B_HEAD_DIM, q, zero), jnp.where(lane >= B_HEAD_DIM, q, zero)], axis=0)


def _lambda(lam_ref, lam_init):
    lq = lam_ref[...]
    a = jnp.sum(lq[0:1, :] * lq[1:2, :], axis=-1, keepdims=True)
    b = jnp.sum(lq[2:3, :] * lq[3:4, :], axis=-1, keepdims=True)
    return jnp.exp(a) - jnp.exp(b) + lam_init


def _diff_finish(acc, l, lam, sg, lam_init):
    t = acc.shape[0] // 2
    o = acc[:t] / l[:t] - lam * (acc[t:] / l[t:])
    return _rms(o, sg) * (1.0 - lam_init)


def _diff_kernel(q_ref, k_ref, v_ref, boff_ref, bdiag_ref, step_ref, lam_ref, sg_ref, o_ref,
                 m_scr, l_scr, acc_scr, *, lam_init):
    i = pl.program_id(2)
    t = B_TILE
    qq = _split_halves(q_ref[0])

    def scores(j, bias):
        kj = k_ref[0, pl.ds(pl.multiple_of(j * t, t), t), :]
        s = _dot_nt(qq, kj).reshape(2, t, t) + bias[None]
        return s.reshape(2 * t, t)

    def pv(e, j):
        vj = v_ref[0, pl.ds(pl.multiple_of(j * t, t), t), :]
        return _dot(e.astype(BF16), vj)

    s = scores(i, bdiag_ref[0])
    m = jnp.max(s, axis=-1, keepdims=True)
    e = jnp.exp(s - m)
    m_scr[...] = m
    l_scr[...] = jnp.sum(e, axis=-1, keepdims=True)
    acc_scr[...] = pv(e, i)

    def body(j, carry):
        shift = step_ref[0, :, 0:1] * (i - j - 1).astype(F32)
        s = scores(j, boff_ref[0]) + shift
        m_old = m_scr[...]
        m_new = jnp.maximum(m_old, jnp.max(s, axis=-1, keepdims=True))
        alpha = jnp.exp(m_old - m_new)
        e = jnp.exp(s - m_new)
        l_scr[...] = alpha * l_scr[...] + jnp.sum(e, axis=-1, keepdims=True)
        acc_scr[...] = alpha * acc_scr[...] + pv(e, j)
        m_scr[...] = m_new
        return carry

    lax.fori_loop(0, i, body, 0)
    lam = _lambda(lam_ref, lam_init)
    o_ref[0] = _diff_finish(acc_scr[...], l_scr[...], lam, sg_ref[...], lam_init).astype(BF16)


def _diff_prompt(qb, kb, vb, boff, bdiag, step, lam_qk, sg, lam_init):
    n, s_len, _ = qb.shape
    t = B_TILE
    assert s_len % t == 0 and t % CHUNK == 0
    qspec = pl.BlockSpec((1, t, B_VDIM), lambda b, h, i: (b, i, h))
    kvspec = pl.BlockSpec((1, s_len, B_VDIM), lambda b, h, i: (b, 0, h))
    hspec = pl.BlockSpec((1, t, t), lambda b, h, i: (h, 0, 0))
    return pl.pallas_call(
        functools.partial(_diff_kernel, lam_init=lam_init),
        grid=(n, B_HEADS, s_len // t),
        in_specs=[qspec, kvspec, kvspec, hspec, hspec,
                  pl.BlockSpec((1, 1, 128), lambda b, h, i: (h, 0, 0)),
                  _const_spec(lam_qk.shape), _const_spec(sg.shape)],
        out_specs=qspec,
        out_shape=jax.ShapeDtypeStruct(qb.shape, BF16),
        scratch_shapes=[pltpu.VMEM((2 * t, 1), F32), pltpu.VMEM((2 * t, 1), F32),
                        pltpu.VMEM((2 * t, B_VDIM), F32)],
        compiler_params=_params("parallel", "parallel", "arbitrary"),
        name="diff_prompt",
    )(qb, kb, vb, boff, bdiag, step, lam_qk, sg)


SAMPLE_KEY_PAD = 128


def _pad_rows(x, rows):
    return jnp.concatenate([x, jnp.zeros((rows - x.shape[0], x.shape[1]), x.dtype)], axis=0)


def _band_sample_kernel(q_ref, kn_ref, vn_ref, kn32_ref, vn32_ref, ck_ref, cv_ref, bias_ref,
                        o_ref, ko_ref, vo_ref):
    t = q_ref.shape[0]
    l_a = ck_ref.shape[1]
    ck = ck_ref[0]
    cv = cv_ref[0]
    ko_ref[0, 0:l_a - t, :] = ck[t:, :]
    ko_ref[0, l_a - t:l_a, :] = kn32_ref[...]
    vo_ref[0, 0:l_a - t, :] = cv[t:, :]
    vo_ref[0, l_a - t:l_a, :] = vn32_ref[...]
    n_keys = bias_ref.shape[2]
    k = jnp.concatenate([ck.astype(BF16), _pad_rows(kn_ref[...], n_keys - l_a)], axis=0)
    v = jnp.concatenate([cv.astype(BF16), _pad_rows(vn_ref[...], n_keys - l_a)], axis=0)
    q = q_ref[...]
    outs = []
    for h in range(A_HEADS):
        sl = slice(h * A_HEAD_DIM, (h + 1) * A_HEAD_DIM)
        s = _dot_nt(q[:, sl], k[:, sl]) + bias_ref[h]
        e = jnp.exp(s - jnp.max(s, axis=-1, keepdims=True))
        l = jnp.sum(e, axis=-1, keepdims=True)
        outs.append(_dot(e.astype(BF16), v[:, sl]) / l)
    o_ref[...] = jnp.concatenate(outs, axis=1).astype(BF16)


def _band_sample(qa, kn, vn, kn32, vn32, cache_k, cache_v, bias, t):
    n, l_a, w = cache_k.shape
    row = pl.BlockSpec((t, w), lambda b: (b, 0))
    cspec = pl.BlockSpec((1, l_a, w), lambda b: (b, 0, 0))
    return pl.pallas_call(
        _band_sample_kernel,
        grid=(n,),
        in_specs=[row, row, row, row, row, cspec, cspec, _const_spec(bias.shape)],
        out_specs=[row, cspec, cspec],
        out_shape=[jax.ShapeDtypeStruct(qa.shape, BF16),
                   jax.ShapeDtypeStruct(cache_k.shape, F32), jax.ShapeDtypeStruct(cache_v.shape, F32)],
        compiler_params=_params("parallel"),
        name="band_sample",
    )(qa, kn, vn, kn32, vn32, cache_k, cache_v, bias)


def _diff_sample_kernel(q_ref, kn_ref, vn_ref, ck_ref, cv_ref, bc_ref, bn_ref, lam_ref, sg_ref, o_ref,
                        *, lam_init):
    t = q_ref.shape[0]
    lam = _lambda(lam_ref, lam_init)
    sg = sg_ref[...]
    outs = []
    for h in range(B_HEADS):
        sl = slice(h * B_VDIM, (h + 1) * B_VDIM)
        qq = _split_halves(q_ref[:, sl])
        kc = ck_ref[0, :, sl].astype(BF16)
        vc = cv_ref[0, :, sl].astype(BF16)
        kn = _pad_rows(kn_ref[:, sl], SAMPLE_KEY_PAD)
        vn = _pad_rows(vn_ref[:, sl], SAMPLE_KEY_PAD)
        s_c = (_dot_nt(qq, kc).reshape(2, t, -1) + bc_ref[h][None]).reshape(2 * t, -1)
        s_n = (_dot_nt(qq, kn).reshape(2, t, -1) + bn_ref[h][None]).reshape(2 * t, -1)
        m = jnp.maximum(jnp.max(s_c, axis=-1, keepdims=True), jnp.max(s_n, axis=-1, keepdims=True))
        e_c = jnp.exp(s_c - m)
        e_n = jnp.exp(s_n - m)
        l = jnp.sum(e_c, axis=-1, keepdims=True) + jnp.sum(e_n, axis=-1, keepdims=True)
        acc = _dot(e_c.astype(BF16), vc) + _dot(e_n.astype(BF16), vn)
        outs.append(_diff_finish(acc, l, lam, sg, lam_init))
    o_ref[...] = jnp.concatenate(outs, axis=1).astype(BF16)


def _diff_sample(qb, kn, vn, cache_k, cache_v, bias_c, bias_n, lam_qk, sg, lam_init, t):
    n, past, w = cache_k.shape
    row = pl.BlockSpec((t, w), lambda b: (b, 0))
    cspec = pl.BlockSpec((1, past, w), lambda b: (b, 0, 0))
    return pl.pallas_call(
        functools.partial(_diff_sample_kernel, lam_init=lam_init),
        grid=(n,),
        in_specs=[row, row, row, cspec, cspec, _const_spec(bias_c.shape), _const_spec(bias_n.shape),
                  _const_spec(lam_qk.shape), _const_spec(sg.shape)],
        out_specs=row,
        out_shape=jax.ShapeDtypeStruct(qb.shape, BF16),
        compiler_params=_params("parallel"),
        name="diff_sample",
    )(qb, kn, vn, cache_k, cache_v, bias_c, bias_n, lam_qk, sg)


def _merge_kernel(h_ref, ya_ref, yb_ref, mk_ref, mv_ref, g_ref, wg_ref, bg_ref, wa_ref, wb_ref, wo_ref,
                  wq_ref, wmo_ref, o_ref, att_scr, *, streams, t):
    h = h_ref[...]
    d = h.shape[1]
    u = _rms(h, g_ref[2:3, :]).astype(BF16)
    gates = jax.nn.sigmoid(_dot(u, wg_ref[...]) + bg_ref[...])
    merged = gates[:, :d] * _dot(ya_ref[...], wa_ref[...]) + gates[:, d:] * _dot(yb_ref[...], wb_ref[...])
    h = h + _rms(_dot(merged.astype(BF16), wo_ref[...]), g_ref[3:4, :])

    xn = _rms(h, g_ref[4:5, :]).astype(BF16)
    dh = d // M_HEADS
    q = (_dot(xn, wq_ref[...]) * dh ** -0.5).astype(BF16)
    for b in range(streams):
        rows = slice(b * t, (b + 1) * t)
        for hd in range(M_HEADS):
            cols = slice(hd * dh, (hd + 1) * dh)
            s = _dot_nt(q[rows, cols], mk_ref[b, :, cols])
            e = jnp.exp(s - jnp.max(s, axis=-1, keepdims=True))
            l = jnp.sum(e, axis=-1, keepdims=True)
            att_scr[rows, cols] = (_dot(e.astype(BF16), mv_ref[b, :, cols]) / l).astype(BF16)
    o_ref[...] = h + _rms(_dot(att_scr[...], wmo_ref[...]), g_ref[5:6, :])


def _merge(h, ya, yb, mk, mv, g, wg, bg, wa, wb, wo, wq, wmo, t):
    rows, d = h.shape
    n_mem = mk.shape[1]
    if t >= ROW_TILE:
        tm, streams = ROW_TILE, 1
        assert t % tm == 0
        per = t // tm
        mspec = pl.BlockSpec((1, n_mem, d), lambda i: (i // per, 0, 0))
    else:
        streams = 8
        tm = streams * t
        mspec = pl.BlockSpec((streams, n_mem, d), lambda i: (i, 0, 0))
    assert rows % tm == 0
    t_in = tm // streams

    def row(width):
        return pl.BlockSpec((tm, width), lambda i: (i, 0))

    return pl.pallas_call(
        functools.partial(_merge_kernel, streams=streams, t=t_in),
        grid=(rows // tm,),
        in_specs=[row(d), row(ya.shape[1]), row(yb.shape[1]), mspec, mspec, _const_spec(g.shape),
                  _const_spec(wg.shape), _const_spec(bg.shape), _const_spec(wa.shape), _const_spec(wb.shape),
                  _const_spec(wo.shape), _const_spec(wq.shape), _const_spec(wmo.shape)],
        out_specs=row(d),
        out_shape=jax.ShapeDtypeStruct((rows, d), F32),
        scratch_shapes=[pltpu.VMEM((tm, d), BF16)],
        compiler_params=_params("parallel"),
        name="merge",
    )(h, ya, yb, mk, mv, g, wg, bg, wa, wb, wo, wq, wmo)


def _band_bias(rel_bias, pos_q, pos_k, extra_invalid=None):
    cq = pos_q[:, None] // CHUNK
    ck = pos_k[None, :] // CHUNK
    ok = (ck <= cq) & (ck >= cq - A_PREV_CHUNKS)
    if extra_invalid is not None:
        ok &= ~extra_invalid[None, :]
    rel = np.clip(pos_q[:, None] - pos_k[None, :], -A_REL_MAX, A_REL_MAX) + A_REL_MAX
    return jnp.where(ok[None], rel_bias.astype(F32)[:, rel], NEG_INF)


def _alibi_slopes():
    return 2.0 ** (-8.0 * np.arange(1, B_HEADS + 1) / B_HEADS)


def _alibi(pos_q, pos_k, invalid=None):
    dist = np.abs(pos_q[:, None] - pos_k[None, :]).astype(np.float64)
    bias = -_alibi_slopes()[:, None, None] * dist
    if invalid is not None:
        bias = np.where(invalid[None], NEG_INF, bias)
    return jnp.asarray(bias, F32)


def kernel(x_prompt, x_sample, cache_a_k, cache_a_v, cache_b_k, cache_b_v, cache_mem_k, cache_mem_v, mem_prompt,
           w_in, w_gate, b_gate, rel_bias, lam_qk, subln_g, w_br_a, w_br_b, w_out, w_mq, w_mkv, w_mo, norm_g,
           ffn1_up, ffn1_down, ffn2_up, ffn2_down):
    depth = w_in.shape[0]
    assert depth == 1
    n_p, s_len, d = x_prompt.shape
    n_s, t_len, _ = x_sample.shape
    past = cache_b_k.shape[2]
    l_a = cache_a_k.shape[2]
    keep_a = min(A_WINDOW, s_len)
    n_mem = mem_prompt.shape[1]
    l = 0
    lam_init = 0.8 - 0.6 * math.exp(-0.3 * l)

    g = norm_g[l]
    bf = lambda w: w[l].astype(BF16)
    up1, down1, up2, down2 = bf(ffn1_up), bf(ffn1_down), bf(ffn2_up), bf(ffn2_down)
    win, wg, wa, wb, wo, wq, wmkv, wmo = (bf(w_in), bf(w_gate), bf(w_br_a), bf(w_br_b), bf(w_out), bf(w_mq),
                                          bf(w_mkv), bf(w_mo))
    bg = b_gate[l][None, :]
    lq = lam_qk[l]
    sg = subln_g[l][None, :]

    r = np.arange(A_QTILE)
    c = np.arange(3 * A_QTILE)
    band_bias_p = _band_bias(rel_bias[l], 2 * A_QTILE + r, c)
    tb = np.arange(B_TILE)
    boff = _alibi(B_TILE + tb, tb)
    bdiag = _alibi(tb, tb, invalid=(tb[None, :] // CHUNK) > (tb[:, None] // CHUNK))
    step = jnp.broadcast_to(jnp.asarray(-_alibi_slopes() * B_TILE, F32)[:, None, None], (B_HEADS, 1, 128))

    xp = x_prompt.reshape(n_p * s_len, d)
    h = _ffn(xp, g, up1, down1, 0, 1)
    qa, ka, va, qb, kb, vb, kb32, vb32 = _qkv(h, g, win, a_f32=False)
    tail = h.reshape(n_p, s_len, d)[:, s_len - keep_a:].reshape(n_p * keep_a, d)
    tail_out = _qkv(tail, g, win, a_f32=True)
    ak_p, av_p = tail_out[8], tail_out[9]
    as3 = lambda a: a.reshape(n_p, s_len, -1)
    ya = _band_prompt(as3(qa), as3(ka), as3(va), band_bias_p)
    yb = _diff_prompt(as3(qb), as3(kb), as3(vb), boff, bdiag, step, lq, sg, lam_init)
    mkv32, mkv16 = _memkv(mem_prompt.reshape(n_p * n_mem, d), g, wmkv)
    mk16 = mkv16[:, :d].reshape(n_p, n_mem, d)
    mv16 = mkv16[:, d:].reshape(n_p, n_mem, d)
    h = _merge(h, ya.reshape(-1, A_WIDTH), yb.reshape(-1, B_WIDTH), mk16, mv16, g, wg, bg, wa, wb, wo, wq, wmo,
               s_len)
    y_p = _ffn(h, g, up2, down2, 7, 8).reshape(n_p, s_len, d)

    assert past // CHUNK == (past + t_len - 1) // CHUNK, "sample queries must share one chunk"
    xs = x_sample.reshape(n_s * t_len, d)
    h = _ffn(xs, g, up1, down1, 0, 1)
    qa, ka, va, qb, kb, vb, kb32_s, vb32_s, ka32, va32 = _qkv(h, g, win, a_f32=True)
    pos_q = past + np.arange(t_len)
    n_keys_a = -(-(l_a + t_len) // 128) * 128
    pos_k = np.concatenate([past - l_a + np.arange(l_a), pos_q, np.zeros(n_keys_a - l_a - t_len, np.int64)])
    pad = np.arange(n_keys_a) >= l_a + t_len
    band_bias_s = _band_bias(rel_bias[l], pos_q, pos_k, extra_invalid=pad | (pos_k < 0))
    ya, ak_s, av_s = _band_sample(qa, ka, va, ka32, va32, cache_a_k[l].reshape(n_s, l_a, A_WIDTH),
                                  cache_a_v[l].reshape(n_s, l_a, A_WIDTH), band_bias_s, t_len)
    bias_c = _alibi(pos_q, np.arange(past))
    pos_n = np.concatenate([pos_q, np.zeros(SAMPLE_KEY_PAD - t_len, np.int64)])
    bias_n = _alibi(pos_q, pos_n, invalid=np.broadcast_to(np.arange(SAMPLE_KEY_PAD) >= t_len,
                                                          (t_len, SAMPLE_KEY_PAD)))
    yb = _diff_sample(qb, kb, vb, cache_b_k[l].reshape(n_s, past, B_WIDTH),
                      cache_b_v[l].reshape(n_s, past, B_WIDTH), bias_c, bias_n, lq, sg, lam_init, t_len)
    mk_s = cache_mem_k[l].reshape(n_s, n_mem, d).astype(BF16)
    mv_s = cache_mem_v[l].reshape(n_s, n_mem, d).astype(BF16)
    h = _merge(h, ya, yb, mk_s, mv_s, g, wg, bg, wa, wb, wo, wq, wmo, t_len)
    y_s = _ffn(h, g, up2, down2, 7, 8).reshape(n_s, t_len, d)

    dh_m = d // M_HEADS
    return (y_p, y_s,
            ak_p.reshape(1, n_p, keep_a, A_HEADS, A_HEAD_DIM), av_p.reshape(1, n_p, keep_a, A_HEADS, A_HEAD_DIM),
            kb32.reshape(1, n_p, s_len, B_HEADS, 2, B_HEAD_DIM), vb32.reshape(1, n_p, s_len, B_HEADS, B_VDIM),
            mkv32[:, :d].reshape(1, n_p, n_mem, M_HEADS, dh_m), mkv32[:, d:].reshape(1, n_p, n_mem, M_HEADS, dh_m),
            ak_s.reshape(1, n_s, l_a, A_HEADS, A_HEAD_DIM), av_s.reshape(1, n_s, l_a, A_HEADS, A_HEAD_DIM),
            kb32_s.reshape(1, n_s, t_len, B_HEADS, 2, B_HEAD_DIM), vb32_s.reshape(1, n_s, t_len, B_HEADS, B_VDIM))
```

```python
import functools
import math

import jax
import jax.numpy as jnp
import numpy as np
from jax import lax
from jax.experimental import pallas as pl
from jax.experimental.pallas import tpu as pltpu

F32 = jnp.float32
BF16 = jnp.bfloat16

CHUNK = 64
A_HEADS = 8
A_HEAD_DIM = 64
A_WIDTH = A_HEADS * A_HEAD_DIM
A_PREV_CHUNKS = 8
A_WINDOW = (A_PREV_CHUNKS + 1) * CHUNK
A_REL_MAX = 2 * CHUNK
B_HEADS = 4
B_HEAD_DIM = 64
B_VDIM = 2 * B_HEAD_DIM
B_WIDTH = B_HEADS * B_VDIM
M_HEADS = 4
NORM_EPS = 1e-6
NEG_INF = -1e30
LOG2E = math.log2(math.e)

V7X_VMEM_BYTES = 64 * 1024 * 1024
VMEM_LIMIT = V7X_VMEM_BYTES - 8 * 1024 * 1024

ROW_TILE = 512
A_QTILE = 256
B_TILE = 256


def _params(*sem):
    return pltpu.CompilerParams(dimension_semantics=sem, vmem_limit_bytes=VMEM_LIMIT)


def _const_spec(shape):
    zeros = (0,) * len(shape)
    return pl.BlockSpec(shape, lambda *_: zeros, pipeline_mode=pl.Buffered(1))


def _row_tile(rows):
    return max(t for t in range(16, min(ROW_TILE, rows) + 1, 16) if rows % t == 0)


def _rms(x, g):
    return x * lax.rsqrt(jnp.mean(x * x, axis=-1, keepdims=True) + NORM_EPS) * g


def _dot(a, b):
    return jnp.dot(a, b, preferred_element_type=F32)


def _dot_nt(a, b):
    return lax.dot_general(a, b, (((1,), (1,)), ((), ())), preferred_element_type=F32)


def _ffn_kernel(x_ref, g_ref, up_ref, down_ref, o_ref, *, pre, post, d_ff):
    x = x_ref[...]
    xn = _rms(x, g_ref[pre:pre + 1, :]).astype(BF16)
    gu = _dot(xn, up_ref[...])
    gate = gu[:, :d_ff]
    act = (gate * jax.nn.sigmoid(gate) * gu[:, d_ff:]).astype(BF16)
    y = _dot(act, down_ref[...])
    o_ref[...] = x + 0.5 * _rms(y, g_ref[post:post + 1, :])


def _ffn(x, g, up, down, pre, post):
    rows, d = x.shape
    d_ff = down.shape[0]
    tm = _row_tile(rows)
    return pl.pallas_call(
        functools.partial(_ffn_kernel, pre=pre, post=post, d_ff=d_ff),
        grid=(rows // tm,),
        in_specs=[pl.BlockSpec((tm, d), lambda i: (i, 0)),
                  _const_spec(g.shape), _const_spec(up.shape), _const_spec(down.shape)],
        out_specs=pl.BlockSpec((tm, d), lambda i: (i, 0)),
        out_shape=jax.ShapeDtypeStruct((rows, d), F32),
        compiler_params=_params("parallel"),
        name="ffn",
    )(x, g, up, down)


def _qkv_kernel(h_ref, g_ref, w_ref, *out_refs, a_f32):
    u = _rms(h_ref[...], g_ref[2:3, :]).astype(BF16)
    y = _dot(u, w_ref[...])
    w = A_WIDTH
    qa, ka, va, qb, kb, vb = (y[:, j * w:(j + 1) * w] for j in range(6))
    scale = A_HEAD_DIM ** -0.5
    out_refs[0][...] = (qa * scale).astype(BF16)
    out_refs[1][...] = ka.astype(BF16)
    out_refs[2][...] = va.astype(BF16)
    out_refs[3][...] = (qb * (scale * LOG2E)).astype(BF16)
    out_refs[4][...] = kb.astype(BF16)
    out_refs[5][...] = vb.astype(BF16)
    out_refs[6][...] = kb
    out_refs[7][...] = vb
    if a_f32:
        out_refs[8][...] = ka
        out_refs[9][...] = va


def _qkv(h, g, w_in, a_f32):
    assert A_WIDTH == B_WIDTH and A_HEAD_DIM == B_HEAD_DIM
    rows, d = h.shape
    tm = _row_tile(rows)
    assert rows % tm == 0
    n_f32 = 4 if a_f32 else 2
    spec = pl.BlockSpec((tm, A_WIDTH), lambda i: (i, 0))
    return pl.pallas_call(
        functools.partial(_qkv_kernel, a_f32=a_f32),
        grid=(rows // tm,),
        in_specs=[pl.BlockSpec((tm, d), lambda i: (i, 0)), _const_spec(g.shape), _const_spec(w_in.shape)],
        out_specs=[spec] * (6 + n_f32),
        out_shape=[jax.ShapeDtypeStruct((rows, A_WIDTH), BF16)] * 6
        + [jax.ShapeDtypeStruct((rows, A_WIDTH), F32)] * n_f32,
        compiler_params=_params("parallel"),
        name="qkv",
    )(h, g, w_in)


def _memkv_kernel(m_ref, g_ref, w_ref, kv_ref, kvb_ref):
    y = _dot(_rms(m_ref[...], g_ref[6:7, :]).astype(BF16), w_ref[...])
    kv_ref[...] = y
    kvb_ref[...] = y.astype(BF16)


def _memkv(mem, g, w_mkv):
    rows, d = mem.shape
    n = w_mkv.shape[1]
    return pl.pallas_call(
        _memkv_kernel,
        grid=(1,),
        in_specs=[_const_spec(mem.shape), _const_spec(g.shape), _const_spec(w_mkv.shape)],
        out_specs=[pl.BlockSpec((rows, n), lambda i: (0, 0))] * 2,
        out_shape=[jax.ShapeDtypeStruct((rows, n), F32), jax.ShapeDtypeStruct((rows, n), BF16)],
        compiler_params=_params("arbitrary"),
        name="memkv",
    )(mem, g, w_mkv)


def _band_kernel(q_ref, k0_ref, k1_ref, k2_ref, v0_ref, v1_ref, v2_ref, wtab_ref, o_ref, bias_scr):
    i = pl.program_id(1)
    q = q_ref[0]
    k = jnp.concatenate([k0_ref[0], k1_ref[0], k2_ref[0]], axis=0)
    v = jnp.concatenate([v0_ref[0], v1_ref[0], v2_ref[0]], axis=0)
    tq, tk = q.shape[0], k.shape[0]
    col = lax.broadcasted_iota(jnp.int32, (tq, tk), 1)

    @pl.when((pl.program_id(0) == 0) & (i == 0))
    def _build_bias():
        width = wtab_ref.shape[1]
        r_w = lax.broadcasted_iota(jnp.int32, (tq, width), 0)
        r = lax.broadcasted_iota(jnp.int32, (tq, tk), 0)
        back = r // CHUNK - col // CHUNK + A_PREV_CHUNKS
        in_band = (back >= 0) & (back <= A_PREV_CHUNKS)
        for h in range(A_HEADS):
            x = jnp.broadcast_to(wtab_ref[h:h + 1, :], (tq, width))
            for bit in range((tq - 1).bit_length()):
                x = jnp.where((r_w >> bit) & 1 == 1, pltpu.roll(x, 1 << bit, 1), x)
            bias_scr[h] = jnp.where(in_band, x[:, :tk], NEG_INF)

    valid = col + (i - 2) * tq >= 0
    outs = []
    for h in range(A_HEADS):
        sl = slice(h * A_HEAD_DIM, (h + 1) * A_HEAD_DIM)
        s = _dot_nt(q[:, sl], k[:, sl]) + bias_scr[h]
        s = jnp.where(valid, s, NEG_INF)
        e = jnp.exp(s - jnp.max(s, axis=-1, keepdims=True))
        l = jnp.sum(e, axis=-1, keepdims=True)
        outs.append(_dot(e.astype(BF16), v[:, sl]) / l)
    o_ref[0] = jnp.concatenate(outs, axis=1).astype(BF16)


def _band_prompt(qa, ka, va, wtab):
    n, s_len, w = qa.shape
    t = A_QTILE
    assert s_len % t == 0 and 2 * t == A_PREV_CHUNKS * CHUNK and wtab.shape[1] >= 4 * t
    qspec = pl.BlockSpec((1, t, w), lambda b, i: (b, i, 0))

    def kspec(back):
        return pl.BlockSpec((1, t, w), lambda b, i: (b, jnp.maximum(i - back, 0), 0))

    return pl.pallas_call(
        _band_kernel,
        grid=(n, s_len // t),
        in_specs=[qspec, kspec(2), kspec(1), kspec(0), kspec(2), kspec(1), kspec(0), _const_spec(wtab.shape)],
        out_specs=qspec,
        out_shape=jax.ShapeDtypeStruct((n, s_len, w), BF16),
        scratch_shapes=[pltpu.VMEM((A_HEADS, t, 3 * t), F32)],
        compiler_params=_params("arbitrary", "arbitrary"),
        name="band_prompt",
    )(qa, ka, ka, ka, va, va, va, wtab)


def _split_halves(q):
    lane = lax.broadcasted_iota(jnp.int32, q.shape, 1)
    zero = jnp.zeros_like(q)
    return jnp.concatenate([jnp.where(lane < B_HEAD_DIM, q, zero), jnp.where(lane >= B_HEAD_DIM, q, zero)], axis=0)


def _lambda(lam_ref, lam_init):
    lq = lam_ref[...]
    a = jnp.sum(lq[0:1, :] * lq[1:2, :], axis=-1, keepdims=True)
    b = jnp.sum(lq[2:3, :] * lq[3:4, :], axis=-1, keepdims=True)
    return jnp.exp(a) - jnp.exp(b) + lam_init


def _diff_finish(acc, l, lam, sg, lam_init):
    t = acc.shape[0] // 2
    o = acc[:t] / l[:t] - lam * (acc[t:] / l[t:])
    return _rms(o, sg) * (1.0 - lam_init)


DIFF_ACC_ROWS = B_VDIM + 16


def _diff_kernel(q_ref, k_ref, v_ref, kbias_ref, dbias_ref, step_ref, lam_ref, sg_ref, o_ref,
                 qt_scr, vt_scr, s0_scr, s1_scr, m_scr, acc_scr, *, lam_init):
    i = pl.program_id(2)
    t = B_TILE
    n_tiles = k_ref.shape[1] // t

    def tile(j):
        return pl.ds(pl.multiple_of(j * t, t), t)

    @pl.when(i == 0)
    def _transpose_values():
        ones_row = (lax.broadcasted_iota(jnp.int32, (DIFF_ACC_ROWS - B_VDIM, t), 0) == 0).astype(BF16)

        def body(j, carry):
            vt_scr[0:B_VDIM, tile(j)] = v_ref[0, tile(j), :].astype(F32).T.astype(BF16)
            vt_scr[B_VDIM:DIFF_ACC_ROWS, tile(j)] = ones_row
            return carry

        lax.fori_loop(0, n_tiles, body, 0)

    qt = q_ref[0].astype(F32).T
    row = lax.broadcasted_iota(jnp.int32, qt.shape, 0)
    qt_scr[...] = jnp.concatenate([jnp.where(row < B_HEAD_DIM, qt, 0.0), jnp.where(row >= B_HEAD_DIM, qt, 0.0)],
                                  axis=1).astype(BF16)
    m_scr[...] = jnp.full(m_scr.shape, NEG_INF, F32)
    acc_scr[...] = jnp.zeros(acc_scr.shape, F32)
    step = step_ref[0, :, 0:1]

    def qk(j, s_scr):
        s_scr[...] = _dot(k_ref[0, tile(j), :], qt_scr[...])

    def softmax_pv(j, s_scr, bias_ref):
        s = s_scr[...] + bias_ref[0]
        m_old = m_scr[...]
        m_new = jnp.maximum(m_old, jnp.max(s, axis=0, keepdims=True))
        p = jnp.exp2(s - m_new).astype(BF16)
        acc_scr[...] = jnp.exp2(m_old - m_new) * acc_scr[...] + _dot(vt_scr[:, tile(j)], p)
        m_scr[...] = m_new + step

    qk(0, s0_scr)

    def pair(jj, carry):
        j = 2 * jj
        qk(j + 1, s1_scr)
        softmax_pv(j, s0_scr, kbias_ref)
        qk(j + 2, s0_scr)
        softmax_pv(j + 1, s1_scr, kbias_ref)
        return carry

    lax.fori_loop(0, i // 2, pair, 0)

    @pl.when(i % 2 == 1)
    def _odd():
        qk(i, s1_scr)
        softmax_pv(i - 1, s0_scr, kbias_ref)
        softmax_pv(i, s1_scr, dbias_ref)

    @pl.when(i % 2 == 0)
    def _even():
        softmax_pv(i, s0_scr, dbias_ref)

    acc = acc_scr[...]
    o = acc[0:B_VDIM, :] / acc[B_VDIM:B_VDIM + 1, :]
    o = o[:, :t] - _lambda(lam_ref, lam_init) * o[:, t:]
    o = o * lax.rsqrt(jnp.mean(o * o, axis=0, keepdims=True) + NORM_EPS) * sg_ref[...] * (1.0 - lam_init)
    o_ref[0] = o.T.astype(BF16)


def _diff_prompt(qb, kb, vb, kbias, dbias, step, lam_qk, sg_col, lam_init):
    n, s_len, _ = qb.shape
    t = B_TILE
    assert s_len % t == 0 and t % CHUNK == 0
    qspec = pl.BlockSpec((1, t, B_VDIM), lambda b, h, i: (b, i, h))
    kvspec = pl.BlockSpec((1, s_len, B_VDIM), lambda b, h, i: (b, 0, h))
    hspec = pl.BlockSpec((1, t, 2 * t), lambda b, h, i: (h, 0, 0))
    return pl.pallas_call(
        functools.partial(_diff_kernel, lam_init=lam_init),
        grid=(n, B_HEADS, s_len // t),
        in_specs=[qspec, kvspec, kvspec, hspec, hspec,
                  pl.BlockSpec((1, 1, 128), lambda b, h, i: (h, 0, 0)),
                  _const_spec(lam_qk.shape), _const_spec(sg_col.shape)],
        out_specs=qspec,
        out_shape=jax.ShapeDtypeStruct(qb.shape, BF16),
        scratch_shapes=[pltpu.VMEM((B_VDIM, 2 * t), BF16), pltpu.VMEM((DIFF_ACC_ROWS, s_len), BF16),
                        pltpu.VMEM((t, 2 * t), F32), pltpu.VMEM((t, 2 * t), F32),
                        pltpu.VMEM((1, 2 * t), F32), pltpu.VMEM((DIFF_ACC_ROWS, 2 * t), F32)],
        compiler_params=_params("parallel", "parallel", "arbitrary"),
        name="diff_prompt",
    )(qb, kb, vb, kbias, dbias, step, lam_qk, sg_col)


SAMPLE_KEY_PAD = 128


def _pad_rows(x, rows):
    return jnp.concatenate([x, jnp.zeros((rows - x.shape[0], x.shape[1]), x.dtype)], axis=0)


def _band_sample_kernel(q_ref, kn_ref, vn_ref, kn32_ref, vn32_ref, ck_ref, cv_ref, bias_ref,
                        o_ref, ko_ref, vo_ref):
    t = q_ref.shape[0]
    l_a = ck_ref.shape[1]
    ck = ck_ref[0]
    cv = cv_ref[0]
    ko_ref[0, 0:l_a - t, :] = ck[t:, :]
    ko_ref[0, l_a - t:l_a, :] = kn32_ref[...]
    vo_ref[0, 0:l_a - t, :] = cv[t:, :]
    vo_ref[0, l_a - t:l_a, :] = vn32_ref[...]
    n_keys = bias_ref.shape[2]
    k = jnp.concatenate([ck.astype(BF16), _pad_rows(kn_ref[...], n_keys - l_a)], axis=0)
    v = jnp.concatenate([cv.astype(BF16), _pad_rows(vn_ref[...], n_keys - l_a)], axis=0)
    q = q_ref[...]
    outs = []
    for h in range(A_HEADS):
        sl = slice(h * A_HEAD_DIM, (h + 1) * A_HEAD_DIM)
        s = _dot_nt(q[:, sl], k[:, sl]) + bias_ref[h]
        e = jnp.exp(s - jnp.max(s, axis=-1, keepdims=True))
        l = jnp.sum(e, axis=-1, keepdims=True)
        outs.append(_dot(e.astype(BF16), v[:, sl]) / l)
    o_ref[...] = jnp.concatenate(outs, axis=1).astype(BF16)


def _band_sample(qa, kn, vn, kn32, vn32, cache_k, cache_v, bias, t):
    n, l_a, w = cache_k.shape
    row = pl.BlockSpec((t, w), lambda b: (b, 0))
    cspec = pl.BlockSpec((1, l_a, w), lambda b: (b, 0, 0))
    return pl.pallas_call(
        _band_sample_kernel,
        grid=(n,),
        in_specs=[row, row, row, row, row, cspec, cspec, _const_spec(bias.shape)],
        out_specs=[row, cspec, cspec],
        out_shape=[jax.ShapeDtypeStruct(qa.shape, BF16),
                   jax.ShapeDtypeStruct(cache_k.shape, F32), jax.ShapeDtypeStruct(cache_v.shape, F32)],
        compiler_params=_params("parallel"),
        name="band_sample",
    )(qa, kn, vn, kn32, vn32, cache_k, cache_v, bias)


def _diff_sample_kernel(q_ref, kn_ref, vn_ref, ck_ref, cv_ref, bc_ref, bn_ref, lam_ref, sg_ref, o_ref,
                        *, lam_init):
    t = q_ref.shape[0]
    lam = _lambda(lam_ref, lam_init)
    sg = sg_ref[...]
    outs = []
    for h in range(B_HEADS):
        sl = slice(h * B_VDIM, (h + 1) * B_VDIM)
        qq = _split_halves(q_ref[:, sl])
        kc = ck_ref[0, :, sl].astype(BF16)
        vc = cv_ref[0, :, sl].astype(BF16)
        kn = _pad_rows(kn_ref[:, sl], SAMPLE_KEY_PAD)
        vn = _pad_rows(vn_ref[:, sl], SAMPLE_KEY_PAD)
        s_c = (_dot_nt(qq, kc).reshape(2, t, -1) + bc_ref[h][None]).reshape(2 * t, -1)
        s_n = (_dot_nt(qq, kn).reshape(2, t, -1) + bn_ref[h][None]).reshape(2 * t, -1)
        m = jnp.maximum(jnp.max(s_c, axis=-1, keepdims=True), jnp.max(s_n, axis=-1, keepdims=True))
        e_c = jnp.exp2(s_c - m)
        e_n = jnp.exp2(s_n - m)
        l = jnp.sum(e_c, axis=-1, keepdims=True) + jnp.sum(e_n, axis=-1, keepdims=True)
        acc = _dot(e_c.astype(BF16), vc) + _dot(e_n.astype(BF16), vn)
        outs.append(_diff_finish(acc, l, lam, sg, lam_init))
    o_ref[...] = jnp.concatenate(outs, axis=1).astype(BF16)


def _diff_sample(qb, kn, vn, cache_k, cache_v, bias_c, bias_n, lam_qk, sg, lam_init, t):
    n, past, w = cache_k.shape
    row = pl.BlockSpec((t, w), lambda b: (b, 0))
    cspec = pl.BlockSpec((1, past, w), lambda b: (b, 0, 0))
    return pl.pallas_call(
        functools.partial(_diff_sample_kernel, lam_init=lam_init),
        grid=(n,),
        in_specs=[row, row, row, cspec, cspec, _const_spec(bias_c.shape), _const_spec(bias_n.shape),
                  _const_spec(lam_qk.shape), _const_spec(sg.shape)],
        out_specs=row,
        out_shape=jax.ShapeDtypeStruct(qb.shape, BF16),
        compiler_params=_params("parallel"),
        name="diff_sample",
    )(qb, kn, vn, cache_k, cache_v, bias_c, bias_n, lam_qk, sg)


def _merge_kernel(h_ref, ya_ref, yb_ref, mk_ref, mv_ref, g_ref, wg_ref, bg_ref, wa_ref, wb_ref, wo_ref,
                  wq_ref, wmo_ref, o_ref, att_scr, *, streams, t):
    h = h_ref[...]
    d = h.shape[1]
    u = _rms(h, g_ref[2:3, :]).astype(BF16)
    gates = jax.nn.sigmoid(_dot(u, wg_ref[...]) + bg_ref[...])
    merged = gates[:, :d] * _dot(ya_ref[...], wa_ref[...]) + gates[:, d:] * _dot(yb_ref[...], wb_ref[...])
    h = h + _rms(_dot(merged.astype(BF16), wo_ref[...]), g_ref[3:4, :])

    xn = _rms(h, g_ref[4:5, :]).astype(BF16)
    dh = d // M_HEADS
    q = (_dot(xn, wq_ref[...]) * dh ** -0.5).astype(BF16)
    for b in range(streams):
        rows = slice(b * t, (b + 1) * t)
        for hd in range(M_HEADS):
            cols = slice(hd * dh, (hd + 1) * dh)
            s = _dot_nt(q[rows, cols], mk_ref[b, :, cols])
            e = jnp.exp(s - jnp.max(s, axis=-1, keepdims=True))
            l = jnp.sum(e, axis=-1, keepdims=True)
            att_scr[rows, cols] = (_dot(e.astype(BF16), mv_ref[b, :, cols]) / l).astype(BF16)
    o_ref[...] = h + _rms(_dot(att_scr[...], wmo_ref[...]), g_ref[5:6, :])


def _merge(h, ya, yb, mk, mv, g, wg, bg, wa, wb, wo, wq, wmo, t):
    rows, d = h.shape
    n_mem = mk.shape[1]
    if t >= ROW_TILE:
        tm, streams = ROW_TILE, 1
        assert t % tm == 0
        per = t // tm
        mspec = pl.BlockSpec((1, n_mem, d), lambda i: (i // per, 0, 0))
    else:
        streams = 8
        tm = streams * t
        mspec = pl.BlockSpec((streams, n_mem, d), lambda i: (i, 0, 0))
    assert rows % tm == 0
    t_in = tm // streams

    def row(width):
        return pl.BlockSpec((tm, width), lambda i: (i, 0))

    return pl.pallas_call(
        functools.partial(_merge_kernel, streams=streams, t=t_in),
        grid=(rows // tm,),
        in_specs=[row(d), row(ya.shape[1]), row(yb.shape[1]), mspec, mspec, _const_spec(g.shape),
                  _const_spec(wg.shape), _const_spec(bg.shape), _const_spec(wa.shape), _const_spec(wb.shape),
                  _const_spec(wo.shape), _const_spec(wq.shape), _const_spec(wmo.shape)],
        out_specs=row(d),
        out_shape=jax.ShapeDtypeStruct((rows, d), F32),
        scratch_shapes=[pltpu.VMEM((tm, d), BF16)],
        compiler_params=_params("parallel"),
        name="merge",
    )(h, ya, yb, mk, mv, g, wg, bg, wa, wb, wo, wq, wmo)


def _band_bias(rel_bias, pos_q, pos_k, extra_invalid=None):
    cq = pos_q[:, None] // CHUNK
    ck = pos_k[None, :] // CHUNK
    ok = (ck <= cq) & (ck >= cq - A_PREV_CHUNKS)
    if extra_invalid is not None:
        ok &= ~extra_invalid[None, :]
    rel = np.clip(pos_q[:, None] - pos_k[None, :], -A_REL_MAX, A_REL_MAX) + A_REL_MAX
    return jnp.where(ok[None], rel_bias.astype(F32)[:, rel], NEG_INF)


def _alibi_slopes():
    return 2.0 ** (-8.0 * np.arange(1, B_HEADS + 1) / B_HEADS)


def _alibi(pos_q, pos_k, invalid=None):
    dist = np.abs(pos_q[:, None] - pos_k[None, :]).astype(np.float64)
    bias = -(_alibi_slopes() * LOG2E)[:, None, None] * dist
    if invalid is not None:
        bias = np.where(invalid[None], NEG_INF, bias)
    return jnp.asarray(bias, F32)


def _diff_prompt_tables():
    sigma = (_alibi_slopes() * LOG2E)[:, None, None]
    c = np.arange(B_TILE)[:, None].astype(np.float64)
    r = np.arange(B_TILE)[None, :].astype(np.float64)
    kbias = np.broadcast_to(sigma * c, (B_HEADS, B_TILE, B_TILE))
    dbias = np.where(c <= r, sigma * c, sigma * (2 * r - c))
    dbias = np.where((c // CHUNK) > (r // CHUNK), NEG_INF, dbias)
    both = lambda x: jnp.asarray(np.concatenate([x, x], axis=2), F32)
    step = jnp.asarray(np.broadcast_to(-sigma * B_TILE, (B_HEADS, 1, 128)), F32)
    return both(kbias), both(dbias), step


def kernel(x_prompt, x_sample, cache_a_k, cache_a_v, cache_b_k, cache_b_v, cache_mem_k, cache_mem_v, mem_prompt,
           w_in, w_gate, b_gate, rel_bias, lam_qk, subln_g, w_br_a, w_br_b, w_out, w_mq, w_mkv, w_mo, norm_g,
           ffn1_up, ffn1_down, ffn2_up, ffn2_down):
    depth = w_in.shape[0]
    assert depth == 1
    n_p, s_len, d = x_prompt.shape
    n_s, t_len, _ = x_sample.shape
    past = cache_b_k.shape[2]
    l_a = cache_a_k.shape[2]
    keep_a = min(A_WINDOW, s_len)
    n_mem = mem_prompt.shape[1]
    l = 0
    lam_init = 0.8 - 0.6 * math.exp(-0.3 * l)

    g = norm_g[l]
    bf = lambda w: w[l].astype(BF16)
    up1, down1, up2, down2 = bf(ffn1_up), bf(ffn1_down), bf(ffn2_up), bf(ffn2_down)
    win, wg, wa, wb, wo, wq, wmkv, wmo = (bf(w_in), bf(w_gate), bf(w_br_a), bf(w_br_b), bf(w_out), bf(w_mq),
                                          bf(w_mkv), bf(w_mo))
    bg = b_gate[l][None, :]
    lq = lam_qk[l]
    sg = subln_g[l][None, :]

    width = 4 * A_QTILE
    cr = (np.arange(width) + A_QTILE) % width - A_QTILE
    band_wtab = rel_bias[l].astype(F32)[:, np.clip(2 * A_QTILE - cr, -A_REL_MAX, A_REL_MAX) + A_REL_MAX]
    kbias, dbias, step = _diff_prompt_tables()

    xp = x_prompt.reshape(n_p * s_len, d)
    h = _ffn(xp, g, up1, down1, 0, 1)
    qa, ka, va, qb, kb, vb, kb32, vb32 = _qkv(h, g, win, a_f32=False)
    tail = h.reshape(n_p, s_len, d)[:, s_len - keep_a:].reshape(n_p * keep_a, d)
    tail_out = _qkv(tail, g, win, a_f32=True)
    ak_p, av_p = tail_out[8], tail_out[9]
    as3 = lambda a: a.reshape(n_p, s_len, -1)
    ya = _band_prompt(as3(qa), as3(ka), as3(va), band_wtab)
    yb = _diff_prompt(as3(qb), as3(kb), as3(vb), kbias, dbias, step, lq, subln_g[l][:, None], lam_init)
    mkv32, mkv16 = _memkv(mem_prompt.reshape(n_p * n_mem, d), g, wmkv)
    mk16 = mkv16[:, :d].reshape(n_p, n_mem, d)
    mv16 = mkv16[:, d:].reshape(n_p, n_mem, d)
    h = _merge(h, ya.reshape(-1, A_WIDTH), yb.reshape(-1, B_WIDTH), mk16, mv16, g, wg, bg, wa, wb, wo, wq, wmo,
               s_len)
    y_p = _ffn(h, g, up2, down2, 7, 8).reshape(n_p, s_len, d)

    assert past // CHUNK == (past + t_len - 1) // CHUNK, "sample queries must share one chunk"
    xs = x_sample.reshape(n_s * t_len, d)
    h = _ffn(xs, g, up1, down1, 0, 1)
    qa, ka, va, qb, kb, vb, kb32_s, vb32_s, ka32, va32 = _qkv(h, g, win, a_f32=True)
    pos_q = past + np.arange(t_len)
    n_keys_a = -(-(l_a + t_len) // 128) * 128
    pos_k = np.concatenate([past - l_a + np.arange(l_a), pos_q, np.zeros(n_keys_a - l_a - t_len, np.int64)])
    pad = np.arange(n_keys_a) >= l_a + t_len
    band_bias_s = _band_bias(rel_bias[l], pos_q, pos_k, extra_invalid=pad | (pos_k < 0))
    ya, ak_s, av_s = _band_sample(qa, ka, va, ka32, va32, cache_a_k[l].reshape(n_s, l_a, A_WIDTH),
                                  cache_a_v[l].reshape(n_s, l_a, A_WIDTH), band_bias_s, t_len)
    bias_c = _alibi(pos_q, np.arange(past))
    pos_n = np.concatenate([pos_q, np.zeros(SAMPLE_KEY_PAD - t_len, np.int64)])
    bias_n = _alibi(pos_q, pos_n, invalid=np.broadcast_to(np.arange(SAMPLE_KEY_PAD) >= t_len,
                                                          (t_len, SAMPLE_KEY_PAD)))
    yb = _diff_sample(qb, kb, vb, cache_b_k[l].reshape(n_s, past, B_WIDTH),
                      cache_b_v[l].reshape(n_s, past, B_WIDTH), bias_c, bias_n, lq, sg, lam_init, t_len)
    mk_s = cache_mem_k[l].reshape(n_s, n_mem, d).astype(BF16)
    mv_s = cache_mem_v[l].reshape(n_s, n_mem, d).astype(BF16)
    h = _merge(h, ya, yb, mk_s, mv_s, g, wg, bg, wa, wb, wo, wq, wmo, t_len)
    y_s = _ffn(h, g, up2, down2, 7, 8).reshape(n_s, t_len, d)

    dh_m = d // M_HEADS
    return (y_p, y_s,
            ak_p.reshape(1, n_p, keep_a, A_HEADS, A_HEAD_DIM), av_p.reshape(1, n_p, keep_a, A_HEADS, A_HEAD_DIM),
            kb32.reshape(1, n_p, s_len, B_HEADS, 2, B_HEAD_DIM), vb32.reshape(1, n_p, s_len, B_HEADS, B_VDIM),
            mkv32[:, :d].reshape(1, n_p, n_mem, M_HEADS, dh_m), mkv32[:, d:].reshape(1, n_p, n_mem, M_HEADS, dh_m),
            ak_s.reshape(1, n_s, l_a, A_HEADS, A_HEAD_DIM), av_s.reshape(1, n_s, l_a, A_HEADS, A_HEAD_DIM),
            kb32_s.reshape(1, n_s, t_len, B_HEADS, 2, B_HEAD_DIM), vb32_s.reshape(1, n_s, t_len, B_HEADS, B_VDIM))
```

```python
import functools
import math

import jax
import jax.numpy as jnp
import numpy as np
from jax import lax
from jax.experimental import pallas as pl
from jax.experimental.pallas import tpu as pltpu

F32 = jnp.float32
BF16 = jnp.bfloat16

CHUNK = 64
A_HEADS = 8
A_HEAD_DIM = 64
A_WIDTH = A_HEADS * A_HEAD_DIM
A_PREV_CHUNKS = 8
A_WINDOW = (A_PREV_CHUNKS + 1) * CHUNK
A_REL_MAX = 2 * CHUNK
B_HEADS = 4
B_HEAD_DIM = 64
B_VDIM = 2 * B_HEAD_DIM
B_WIDTH = B_HEADS * B_VDIM
M_HEADS = 4
NORM_EPS = 1e-6
NEG_INF = -1e30
LOG2E = math.log2(math.e)

V7X_VMEM_BYTES = 64 * 1024 * 1024
VMEM_LIMIT = V7X_VMEM_BYTES - 8 * 1024 * 1024

ROW_TILE = 512
A_QTILE = 256
B_TILE = 256


def _params(*sem):
    return pltpu.CompilerParams(dimension_semantics=sem, vmem_limit_bytes=VMEM_LIMIT)


def _const_spec(shape):
    zeros = (0,) * len(shape)
    return pl.BlockSpec(shape, lambda *_: zeros, pipeline_mode=pl.Buffered(1))


def _row_tile(rows):
    return max(t for t in range(16, min(ROW_TILE, rows) + 1, 16) if rows % t == 0)


def _rms(x, g):
    return x * lax.rsqrt(jnp.mean(x * x, axis=-1, keepdims=True) + NORM_EPS) * g


def _dot(a, b):
    return jnp.dot(a, b, preferred_element_type=F32)


def _dot_nt(a, b):
    return lax.dot_general(a, b, (((1,), (1,)), ((), ())), preferred_element_type=F32)


def _ffn_kernel(x_ref, g_ref, up_ref, down_ref, o_ref, *, pre, post, d_ff):
    x = x_ref[...]
    xn = _rms(x, g_ref[pre:pre + 1, :]).astype(BF16)
    gu = _dot(xn, up_ref[...])
    gate = gu[:, :d_ff]
    act = (gate * jax.nn.sigmoid(gate) * gu[:, d_ff:]).astype(BF16)
    y = _dot(act, down_ref[...])
    o_ref[...] = x + 0.5 * _rms(y, g_ref[post:post + 1, :])


def _ffn(x, g, up, down, pre, post):
    rows, d = x.shape
    d_ff = down.shape[0]
    tm = _row_tile(rows)
    return pl.pallas_call(
        functools.partial(_ffn_kernel, pre=pre, post=post, d_ff=d_ff),
        grid=(rows // tm,),
        in_specs=[pl.BlockSpec((tm, d), lambda i: (i, 0)),
                  _const_spec(g.shape), _const_spec(up.shape), _const_spec(down.shape)],
        out_specs=pl.BlockSpec((tm, d), lambda i: (i, 0)),
        out_shape=jax.ShapeDtypeStruct((rows, d), F32),
        compiler_params=_params("parallel"),
        name="ffn",
    )(x, g, up, down)


STAT_ROWS = 8


def _group_norm_max(x16, grp_ref):
    x = x16.astype(F32)
    sq = _dot((x * x).astype(BF16), grp_ref[...])
    blocks = [jnp.max(sq[r:r + B_TILE], axis=0, keepdims=True) for r in range(0, x.shape[0], B_TILE)]
    return jnp.concatenate(blocks + [jnp.zeros((STAT_ROWS - len(blocks), sq.shape[1]), F32)], axis=0)


def _qkv_kernel(h_ref, g_ref, w_ref, *refs, mode):
    u = _rms(h_ref[...], g_ref[2:3, :]).astype(BF16)
    y = _dot(u, w_ref[...])
    w = A_WIDTH
    qa, ka, va, qb, kb, vb = (y[:, j * w:(j + 1) * w] for j in range(6))
    if mode == "tail":
        refs[0][...] = ka
        refs[1][...] = va
        return
    out = refs[1:] if mode == "prompt" else refs
    scale = A_HEAD_DIM ** -0.5
    qb16 = (qb * (scale * LOG2E)).astype(BF16)
    kb16 = kb.astype(BF16)
    out[0][...] = (qa * scale).astype(BF16)
    out[1][...] = ka.astype(BF16)
    out[2][...] = va.astype(BF16)
    out[3][...] = qb16
    out[4][...] = kb16
    out[5][...] = vb.astype(BF16)
    out[7][...] = vb
    if mode == "prompt":
        out[6][0] = kb.T
        out[8][...] = _group_norm_max(qb16, refs[0])
        out[9][...] = _group_norm_max(kb16, refs[0])
    else:
        out[6][...] = kb
        out[8][...] = ka
        out[9][...] = va


def _qkv(h, g, w_in, mode, s_len=None):
    assert A_WIDTH == B_WIDTH and A_HEAD_DIM == B_HEAD_DIM
    rows, d = h.shape
    tm = _row_tile(rows)
    spec = pl.BlockSpec((tm, A_WIDTH), lambda i: (i, 0))
    f32 = jax.ShapeDtypeStruct((rows, A_WIDTH), F32)
    b16 = jax.ShapeDtypeStruct((rows, A_WIDTH), BF16)
    in_specs = [pl.BlockSpec((tm, d), lambda i: (i, 0)), _const_spec(g.shape), _const_spec(w_in.shape)]
    args = [h, g, w_in]
    if mode == "tail":
        out_specs, out_shape = [spec] * 2, [f32] * 2
    elif mode == "sample":
        out_specs, out_shape = [spec] * 10, [b16] * 6 + [f32] * 4
    else:
        assert s_len % tm == 0 and tm % B_TILE == 0 and tm // B_TILE <= STAT_ROWS
        per = s_len // tm
        grp = (np.arange(A_WIDTH)[:, None] // B_HEAD_DIM == np.arange(128)[None, :]).astype(np.float32)
        in_specs.append(_const_spec(grp.shape))
        args.append(jnp.asarray(grp, BF16))
        stat_spec = pl.BlockSpec((STAT_ROWS, 128), lambda i: (i, 0))
        stat = jax.ShapeDtypeStruct((rows // tm * STAT_ROWS, 128), F32)
        kt_spec = pl.BlockSpec((1, A_WIDTH, tm), lambda i: (i // per, 0, i % per))
        kt = jax.ShapeDtypeStruct((rows // s_len, A_WIDTH, s_len), F32)
        out_specs = [spec] * 6 + [kt_spec, spec, stat_spec, stat_spec]
        out_shape = [b16] * 6 + [kt, f32, stat, stat]
    return pl.pallas_call(
        functools.partial(_qkv_kernel, mode=mode),
        grid=(rows // tm,),
        in_specs=in_specs,
        out_specs=out_specs,
        out_shape=out_shape,
        compiler_params=_params("parallel"),
        name="qkv_" + mode,
    )(*args)


def _memkv_kernel(m_ref, g_ref, w_ref, kv_ref, kvb_ref):
    y = _dot(_rms(m_ref[...], g_ref[6:7, :]).astype(BF16), w_ref[...])
    kv_ref[...] = y
    kvb_ref[...] = y.astype(BF16)


def _memkv(mem, g, w_mkv):
    rows, d = mem.shape
    n = w_mkv.shape[1]
    return pl.pallas_call(
        _memkv_kernel,
        grid=(1,),
        in_specs=[_const_spec(mem.shape), _const_spec(g.shape), _const_spec(w_mkv.shape)],
        out_specs=[pl.BlockSpec((rows, n), lambda i: (0, 0))] * 2,
        out_shape=[jax.ShapeDtypeStruct((rows, n), F32), jax.ShapeDtypeStruct((rows, n), BF16)],
        compiler_params=_params("arbitrary"),
        name="memkv",
    )(mem, g, w_mkv)


def _band_kernel(q_ref, k0_ref, k1_ref, k2_ref, v0_ref, v1_ref, v2_ref, wtab_ref, o_ref, bias_scr):
    i = pl.program_id(1)
    q = q_ref[0]
    k = jnp.concatenate([k0_ref[0], k1_ref[0], k2_ref[0]], axis=0)
    v = jnp.concatenate([v0_ref[0], v1_ref[0], v2_ref[0]], axis=0)
    tq, tk = q.shape[0], k.shape[0]
    col = lax.broadcasted_iota(jnp.int32, (tq, tk), 1)

    @pl.when((pl.program_id(0) == 0) & (i == 0))
    def _build_bias():
        width = wtab_ref.shape[1]
        r_w = lax.broadcasted_iota(jnp.int32, (tq, width), 0)
        r = lax.broadcasted_iota(jnp.int32, (tq, tk), 0)
        back = r // CHUNK - col // CHUNK + A_PREV_CHUNKS
        in_band = (back >= 0) & (back <= A_PREV_CHUNKS)
        for h in range(A_HEADS):
            x = jnp.broadcast_to(wtab_ref[h:h + 1, :], (tq, width))
            for bit in range((tq - 1).bit_length()):
                x = jnp.where((r_w >> bit) & 1 == 1, pltpu.roll(x, 1 << bit, 1), x)
            bias_scr[h] = jnp.where(in_band, x[:, :tk], NEG_INF)

    valid = col + (i - 2) * tq >= 0
    outs = []
    for h in range(A_HEADS):
        sl = slice(h * A_HEAD_DIM, (h + 1) * A_HEAD_DIM)
        s = _dot_nt(q[:, sl], k[:, sl]) + bias_scr[h]
        s = jnp.where(valid, s, NEG_INF)
        e = jnp.exp(s - jnp.max(s, axis=-1, keepdims=True))
        l = jnp.sum(e, axis=-1, keepdims=True)
        outs.append(_dot(e.astype(BF16), v[:, sl]) / l)
    o_ref[0] = jnp.concatenate(outs, axis=1).astype(BF16)


def _band_prompt(qa, ka, va, wtab):
    n, s_len, w = qa.shape
    t = A_QTILE
    assert s_len % t == 0 and 2 * t == A_PREV_CHUNKS * CHUNK and wtab.shape[1] >= 4 * t
    qspec = pl.BlockSpec((1, t, w), lambda b, i: (b, i, 0))

    def kspec(back):
        return pl.BlockSpec((1, t, w), lambda b, i: (b, jnp.maximum(i - back, 0), 0))

    return pl.pallas_call(
        _band_kernel,
        grid=(n, s_len // t),
        in_specs=[qspec, kspec(2), kspec(1), kspec(0), kspec(2), kspec(1), kspec(0), _const_spec(wtab.shape)],
        out_specs=qspec,
        out_shape=jax.ShapeDtypeStruct((n, s_len, w), BF16),
        scratch_shapes=[pltpu.VMEM((A_HEADS, t, 3 * t), F32)],
        compiler_params=_params("arbitrary", "arbitrary"),
        name="band_prompt",
    )(qa, ka, ka, ka, va, va, va, wtab)


def _split_halves(q):
    lane = lax.broadcasted_iota(jnp.int32, q.shape, 1)
    zero = jnp.zeros_like(q)
    return jnp.concatenate([jnp.where(lane < B_HEAD_DIM, q, zero), jnp.where(lane >= B_HEAD_DIM, q, zero)], axis=0)


def _lambda(lam_ref, lam_init):
    lq = lam_ref[...]
    a = jnp.sum(lq[0:1, :] * lq[1:2, :], axis=-1, keepdims=True)
    b = jnp.sum(lq[2:3, :] * lq[3:4, :], axis=-1, keepdims=True)
    return jnp.exp(a) - jnp.exp(b) + lam_init


def _diff_finish(acc, l, lam, sg, lam_init):
    t = acc.shape[0] // 2
    o = acc[:t] / l[:t] - lam * (acc[t:] / l[t:])
    return _rms(o, sg) * (1.0 - lam_init)


DIFF_ACC_ROWS = B_VDIM + 16


def _diff_kernel(first_ref, q_ref, k_ref, v_ref, kbias_ref, dbias_ref, step_ref, lam_ref, sg_ref, o_ref,
                 qt_scr, vt_scr, s0_scr, s1_scr, m_scr, acc_scr, *, lam_init):
    i = pl.program_id(2)
    t = B_TILE
    n_tiles = k_ref.shape[1] // t
    first = first_ref[(pl.program_id(0) * B_HEADS + pl.program_id(1)) * n_tiles + i]

    def tile(j):
        return pl.ds(pl.multiple_of(j * t, t), t)

    @pl.when(i == 0)
    def _transpose_values():
        ones_row = (lax.broadcasted_iota(jnp.int32, (DIFF_ACC_ROWS - B_VDIM, t), 0) == 0).astype(BF16)

        def body(j, carry):
            vt_scr[0:B_VDIM, tile(j)] = v_ref[0, tile(j), :].astype(F32).T.astype(BF16)
            vt_scr[B_VDIM:DIFF_ACC_ROWS, tile(j)] = ones_row
            return carry

        lax.fori_loop(0, n_tiles, body, 0)

    qt = q_ref[0].astype(F32).T
    row = lax.broadcasted_iota(jnp.int32, qt.shape, 0)
    qt_scr[...] = jnp.concatenate([jnp.where(row < B_HEAD_DIM, qt, 0.0), jnp.where(row >= B_HEAD_DIM, qt, 0.0)],
                                  axis=1).astype(BF16)
    m_scr[...] = jnp.full(m_scr.shape, NEG_INF, F32)
    acc_scr[...] = jnp.zeros(acc_scr.shape, F32)
    step = step_ref[0, :, 0:1]

    def qk(j, s_scr):
        s_scr[...] = _dot(k_ref[0, tile(j), :], qt_scr[...])

    def softmax_pv(j, s_scr, bias_ref):
        s = s_scr[...] + bias_ref[0]
        m_old = m_scr[...]
        m_new = jnp.maximum(m_old, jnp.max(s, axis=0, keepdims=True))
        p = jnp.exp2(s - m_new).astype(BF16)
        acc_scr[...] = jnp.exp2(m_old - m_new) * acc_scr[...] + _dot(vt_scr[:, tile(j)], p)
        m_scr[...] = m_new + step

    n_before = i - first
    qk(first, s0_scr)

    def pair(jj, carry):
        j = first + 2 * jj
        qk(j + 1, s1_scr)
        softmax_pv(j, s0_scr, kbias_ref)
        qk(j + 2, s0_scr)
        softmax_pv(j + 1, s1_scr, kbias_ref)
        return carry

    lax.fori_loop(0, n_before // 2, pair, 0)

    @pl.when(n_before % 2 == 1)
    def _odd():
        qk(i, s1_scr)
        softmax_pv(i - 1, s0_scr, kbias_ref)
        softmax_pv(i, s1_scr, dbias_ref)

    @pl.when(n_before % 2 == 0)
    def _even():
        softmax_pv(i, s0_scr, dbias_ref)

    acc = acc_scr[...]
    o = acc[0:B_VDIM, :] / acc[B_VDIM:B_VDIM + 1, :]
    o = o[:, :t] - _lambda(lam_ref, lam_init) * o[:, t:]
    o = o * lax.rsqrt(jnp.mean(o * o, axis=0, keepdims=True) + NORM_EPS) * sg_ref[...] * (1.0 - lam_init)
    o_ref[0] = o.T.astype(BF16)


F32_EXP2_UNDERFLOW = 150.0


def _first_key_tile(qstat, kstat, n, s_len, stat_tile):
    n_tiles = s_len // B_TILE
    per = stat_tile // B_TILE
    groups = 2 * B_HEADS

    def norms(stat):
        sq = stat.reshape(n, s_len // stat_tile, STAT_ROWS, 128)[:, :, :per, :groups]
        return jnp.sqrt(jnp.max(sq.reshape(n, n_tiles, B_HEADS, 2), axis=-1))

    q_max = norms(qstat)
    k_max = jnp.max(norms(kstat), axis=1, keepdims=True)
    sigma = jnp.asarray(_alibi_slopes() * LOG2E * B_TILE, F32)
    far = (F32_EXP2_UNDERFLOW + 8.0 + 2.06 * q_max * k_max) / sigma
    far = jnp.where(jnp.isfinite(far), jnp.minimum(far, float(n_tiles)), float(n_tiles))
    i = jnp.arange(n_tiles, dtype=jnp.int32)[None, :, None]
    first = jnp.maximum(i - 1 - jnp.floor(far).astype(jnp.int32), 0)
    return first.transpose(0, 2, 1).reshape(-1)


def _diff_prompt(first, qb, kb, vb, kbias, dbias, step, lam_qk, sg_col, lam_init):
    n, s_len, _ = qb.shape
    t = B_TILE
    assert s_len % t == 0 and t % CHUNK == 0
    qspec = pl.BlockSpec((1, t, B_VDIM), lambda b, h, i, first: (b, i, h))
    kvspec = pl.BlockSpec((1, s_len, B_VDIM), lambda b, h, i, first: (b, 0, h))
    hspec = pl.BlockSpec((1, t, 2 * t), lambda b, h, i, first: (h, 0, 0))
    return pl.pallas_call(
        functools.partial(_diff_kernel, lam_init=lam_init),
        grid_spec=pltpu.PrefetchScalarGridSpec(
            num_scalar_prefetch=1,
            grid=(n, B_HEADS, s_len // t),
            in_specs=[qspec, kvspec, kvspec, hspec, hspec,
                      pl.BlockSpec((1, 1, 128), lambda b, h, i, first: (h, 0, 0)),
                      _const_spec(lam_qk.shape), _const_spec(sg_col.shape)],
            out_specs=qspec,
            scratch_shapes=[pltpu.VMEM((B_VDIM, 2 * t), BF16), pltpu.VMEM((DIFF_ACC_ROWS, s_len), BF16),
                            pltpu.VMEM((t, 2 * t), F32), pltpu.VMEM((t, 2 * t), F32),
                            pltpu.VMEM((1, 2 * t), F32), pltpu.VMEM((DIFF_ACC_ROWS, 2 * t), F32)]),
        out_shape=jax.ShapeDtypeStruct(qb.shape, BF16),
        compiler_params=_params("parallel", "parallel", "arbitrary"),
        name="diff_prompt",
    )(first, qb, kb, vb, kbias, dbias, step, lam_qk, sg_col)


SAMPLE_KEY_PAD = 128


def _pad_rows(x, rows):
    return jnp.concatenate([x, jnp.zeros((rows - x.shape[0], x.shape[1]), x.dtype)], axis=0)


def _band_sample_kernel(q_ref, kn_ref, vn_ref, kn32_ref, vn32_ref, ck_ref, cv_ref, bias_ref,
                        o_ref, ko_ref, vo_ref):
    t = q_ref.shape[0]
    l_a = ck_ref.shape[1]
    ck = ck_ref[0]
    cv = cv_ref[0]
    ko_ref[0, 0:l_a - t, :] = ck[t:, :]
    ko_ref[0, l_a - t:l_a, :] = kn32_ref[...]
    vo_ref[0, 0:l_a - t, :] = cv[t:, :]
    vo_ref[0, l_a - t:l_a, :] = vn32_ref[...]
    n_keys = bias_ref.shape[2]
    k = jnp.concatenate([ck.astype(BF16), _pad_rows(kn_ref[...], n_keys - l_a)], axis=0)
    v = jnp.concatenate([cv.astype(BF16), _pad_rows(vn_ref[...], n_keys - l_a)], axis=0)
    q = q_ref[...]
    outs = []
    for h in range(A_HEADS):
        sl = slice(h * A_HEAD_DIM, (h + 1) * A_HEAD_DIM)
        s = _dot_nt(q[:, sl], k[:, sl]) + bias_ref[h]
        e = jnp.exp(s - jnp.max(s, axis=-1, keepdims=True))
        l = jnp.sum(e, axis=-1, keepdims=True)
        outs.append(_dot(e.astype(BF16), v[:, sl]) / l)
    o_ref[...] = jnp.concatenate(outs, axis=1).astype(BF16)


def _band_sample(qa, kn, vn, kn32, vn32, cache_k, cache_v, bias, t):
    n, l_a, w = cache_k.shape
    row = pl.BlockSpec((t, w), lambda b: (b, 0))
    cspec = pl.BlockSpec((1, l_a, w), lambda b: (b, 0, 0))
    return pl.pallas_call(
        _band_sample_kernel,
        grid=(n,),
        in_specs=[row, row, row, row, row, cspec, cspec, _const_spec(bias.shape)],
        out_specs=[row, cspec, cspec],
        out_shape=[jax.ShapeDtypeStruct(qa.shape, BF16),
                   jax.ShapeDtypeStruct(cache_k.shape, F32), jax.ShapeDtypeStruct(cache_v.shape, F32)],
        compiler_params=_params("parallel"),
        name="band_sample",
    )(qa, kn, vn, kn32, vn32, cache_k, cache_v, bias)


def _diff_sample_kernel(q_ref, kn_ref, vn_ref, ck_ref, cv_ref, bc_ref, bn_ref, lam_ref, sg_ref, o_ref,
                        *, lam_init):
    t = q_ref.shape[0]
    past = ck_ref.shape[2]
    lam = _lambda(lam_ref, lam_init)
    sg = sg_ref[...]
    outs = []
    for h in range(B_HEADS):
        sl = slice(h * B_VDIM, (h + 1) * B_VDIM)
        qq = _split_halves(q_ref[:, sl])
        kc = ck_ref[0, sl, :].astype(BF16)
        vc = cv_ref[0, pl.ds(h, past, stride=B_HEADS), :].astype(BF16)
        kn = _pad_rows(kn_ref[:, sl], SAMPLE_KEY_PAD)
        vn = _pad_rows(vn_ref[:, sl], SAMPLE_KEY_PAD)
        s_c = (_dot(qq, kc).reshape(2, t, -1) + bc_ref[h][None]).reshape(2 * t, -1)
        s_n = (_dot_nt(qq, kn).reshape(2, t, -1) + bn_ref[h][None]).reshape(2 * t, -1)
        m = jnp.maximum(jnp.max(s_c, axis=-1, keepdims=True), jnp.max(s_n, axis=-1, keepdims=True))
        e_c = jnp.exp2(s_c - m)
        e_n = jnp.exp2(s_n - m)
        l = jnp.sum(e_c, axis=-1, keepdims=True) + jnp.sum(e_n, axis=-1, keepdims=True)
        acc = _dot(e_c.astype(BF16), vc) + _dot(e_n.astype(BF16), vn)
        outs.append(_diff_finish(acc, l, lam, sg, lam_init))
    o_ref[...] = jnp.concatenate(outs, axis=1).astype(BF16)


def _diff_sample(qb, kn, vn, cache_kt, cache_v, bias_c, bias_n, lam_qk, sg, lam_init, t):
    n, w, past = cache_kt.shape
    row = pl.BlockSpec((t, w), lambda b: (b, 0))
    return pl.pallas_call(
        functools.partial(_diff_sample_kernel, lam_init=lam_init),
        grid=(n,),
        in_specs=[row, row, row, pl.BlockSpec((1, w, past), lambda b: (b, 0, 0)),
                  pl.BlockSpec((1, past * B_HEADS, B_VDIM), lambda b: (b, 0, 0)),
                  _const_spec(bias_c.shape), _const_spec(bias_n.shape),
                  _const_spec(lam_qk.shape), _const_spec(sg.shape)],
        out_specs=row,
        out_shape=jax.ShapeDtypeStruct(qb.shape, BF16),
        compiler_params=_params("parallel"),
        name="diff_sample",
    )(qb, kn, vn, cache_kt, cache_v, bias_c, bias_n, lam_qk, sg)


def _merge_kernel(h_ref, ya_ref, yb_ref, mk_ref, mv_ref, g_ref, wg_ref, bg_ref, wa_ref, wb_ref, wo_ref,
                  wq_ref, wmo_ref, o_ref, att_scr, *, streams, t):
    h = h_ref[...]
    d = h.shape[1]
    u = _rms(h, g_ref[2:3, :]).astype(BF16)
    gates = jax.nn.sigmoid(_dot(u, wg_ref[...]) + bg_ref[...])
    merged = gates[:, :d] * _dot(ya_ref[...], wa_ref[...]) + gates[:, d:] * _dot(yb_ref[...], wb_ref[...])
    h = h + _rms(_dot(merged.astype(BF16), wo_ref[...]), g_ref[3:4, :])

    xn = _rms(h, g_ref[4:5, :]).astype(BF16)
    dh = d // M_HEADS
    q = (_dot(xn, wq_ref[...]) * dh ** -0.5).astype(BF16)
    for b in range(streams):
        rows = slice(b * t, (b + 1) * t)
        for hd in range(M_HEADS):
            cols = slice(hd * dh, (hd + 1) * dh)
            s = _dot_nt(q[rows, cols], mk_ref[b, :, cols])
            e = jnp.exp(s - jnp.max(s, axis=-1, keepdims=True))
            l = jnp.sum(e, axis=-1, keepdims=True)
            att_scr[rows, cols] = (_dot(e.astype(BF16), mv_ref[b, :, cols]) / l).astype(BF16)
    o_ref[...] = h + _rms(_dot(att_scr[...], wmo_ref[...]), g_ref[5:6, :])


def _merge(h, ya, yb, mk, mv, g, wg, bg, wa, wb, wo, wq, wmo, t):
    rows, d = h.shape
    n_mem = mk.shape[1]
    if t >= ROW_TILE:
        tm, streams = ROW_TILE, 1
        assert t % tm == 0
        per = t // tm
        mspec = pl.BlockSpec((1, n_mem, d), lambda i: (i // per, 0, 0))
    else:
        streams = 8
        tm = streams * t
        mspec = pl.BlockSpec((streams, n_mem, d), lambda i: (i, 0, 0))
    assert rows % tm == 0
    t_in = tm // streams

    def row(width):
        return pl.BlockSpec((tm, width), lambda i: (i, 0))

    return pl.pallas_call(
        functools.partial(_merge_kernel, streams=streams, t=t_in),
        grid=(rows // tm,),
        in_specs=[row(d), row(ya.shape[1]), row(yb.shape[1]), mspec, mspec, _const_spec(g.shape),
                  _const_spec(wg.shape), _const_spec(bg.shape), _const_spec(wa.shape), _const_spec(wb.shape),
                  _const_spec(wo.shape), _const_spec(wq.shape), _const_spec(wmo.shape)],
        out_specs=row(d),
        out_shape=jax.ShapeDtypeStruct((rows, d), F32),
        scratch_shapes=[pltpu.VMEM((tm, d), BF16)],
        compiler_params=_params("parallel"),
        name="merge",
    )(h, ya, yb, mk, mv, g, wg, bg, wa, wb, wo, wq, wmo)


def _band_bias(rel_bias, pos_q, pos_k, extra_invalid=None):
    cq = pos_q[:, None] // CHUNK
    ck = pos_k[None, :] // CHUNK
    ok = (ck <= cq) & (ck >= cq - A_PREV_CHUNKS)
    if extra_invalid is not None:
        ok &= ~extra_invalid[None, :]
    rel = np.clip(pos_q[:, None] - pos_k[None, :], -A_REL_MAX, A_REL_MAX) + A_REL_MAX
    return jnp.where(ok[None], rel_bias.astype(F32)[:, rel], NEG_INF)


def _alibi_slopes():
    return 2.0 ** (-8.0 * np.arange(1, B_HEADS + 1) / B_HEADS)


def _alibi(pos_q, pos_k, invalid=None):
    dist = np.abs(pos_q[:, None] - pos_k[None, :]).astype(np.float64)
    bias = -(_alibi_slopes() * LOG2E)[:, None, None] * dist
    if invalid is not None:
        bias = np.where(invalid[None], NEG_INF, bias)
    return jnp.asarray(bias, F32)


def _diff_prompt_tables():
    sigma = (_alibi_slopes() * LOG2E)[:, None, None]
    c = np.arange(B_TILE)[:, None].astype(np.float64)
    r = np.arange(B_TILE)[None, :].astype(np.float64)
    kbias = np.broadcast_to(sigma * c, (B_HEADS, B_TILE, B_TILE))
    dbias = np.where(c <= r, sigma * c, sigma * (2 * r - c))
    dbias = np.where((c // CHUNK) > (r // CHUNK), NEG_INF, dbias)
    both = lambda x: jnp.asarray(np.concatenate([x, x], axis=2), F32)
    step = jnp.asarray(np.broadcast_to(-sigma * B_TILE, (B_HEADS, 1, 128)), F32)
    return both(kbias), both(dbias), step


def kernel(x_prompt, x_sample, cache_a_k, cache_a_v, cache_b_k, cache_b_v, cache_mem_k, cache_mem_v, mem_prompt,
           w_in, w_gate, b_gate, rel_bias, lam_qk, subln_g, w_br_a, w_br_b, w_out, w_mq, w_mkv, w_mo, norm_g,
           ffn1_up, ffn1_down, ffn2_up, ffn2_down):
    depth = w_in.shape[0]
    assert depth == 1
    n_p, s_len, d = x_prompt.shape
    n_s, t_len, _ = x_sample.shape
    past = cache_b_k.shape[2]
    l_a = cache_a_k.shape[2]
    keep_a = min(A_WINDOW, s_len)
    n_mem = mem_prompt.shape[1]
    l = 0
    lam_init = 0.8 - 0.6 * math.exp(-0.3 * l)

    g = norm_g[l]
    bf = lambda w: w[l].astype(BF16)
    up1, down1, up2, down2 = bf(ffn1_up), bf(ffn1_down), bf(ffn2_up), bf(ffn2_down)
    win, wg, wa, wb, wo, wq, wmkv, wmo = (bf(w_in), bf(w_gate), bf(w_br_a), bf(w_br_b), bf(w_out), bf(w_mq),
                                          bf(w_mkv), bf(w_mo))
    bg = b_gate[l][None, :]
    lq = lam_qk[l]
    sg = subln_g[l][None, :]

    width = 4 * A_QTILE
    cr = (np.arange(width) + A_QTILE) % width - A_QTILE
    band_wtab = rel_bias[l].astype(F32)[:, np.clip(2 * A_QTILE - cr, -A_REL_MAX, A_REL_MAX) + A_REL_MAX]
    kbias, dbias, step = _diff_prompt_tables()

    xp = x_prompt.reshape(n_p * s_len, d)
    h = _ffn(xp, g, up1, down1, 0, 1)
    qa, ka, va, qb, kb, vb, kbt32, vb32, qstat, kstat = _qkv(h, g, win, "prompt", s_len)
    tail = h.reshape(n_p, s_len, d)[:, s_len - keep_a:].reshape(n_p * keep_a, d)
    ak_p, av_p = _qkv(tail, g, win, "tail")
    as3 = lambda a: a.reshape(n_p, s_len, -1)
    ya = _band_prompt(as3(qa), as3(ka), as3(va), band_wtab)
    first = _first_key_tile(qstat, kstat, n_p, s_len, _row_tile(n_p * s_len))
    yb = _diff_prompt(first, as3(qb), as3(kb), as3(vb), kbias, dbias, step, lq, subln_g[l][:, None], lam_init)
    kb32 = kbt32.reshape(n_p, B_HEADS, 2, B_HEAD_DIM, s_len).transpose(0, 4, 1, 2, 3)
    mkv32, mkv16 = _memkv(mem_prompt.reshape(n_p * n_mem, d), g, wmkv)
    mk16 = mkv16[:, :d].reshape(n_p, n_mem, d)
    mv16 = mkv16[:, d:].reshape(n_p, n_mem, d)
    h = _merge(h, ya.reshape(-1, A_WIDTH), yb.reshape(-1, B_WIDTH), mk16, mv16, g, wg, bg, wa, wb, wo, wq, wmo,
               s_len)
    y_p = _ffn(h, g, up2, down2, 7, 8).reshape(n_p, s_len, d)

    assert past // CHUNK == (past + t_len - 1) // CHUNK, "sample queries must share one chunk"
    xs = x_sample.reshape(n_s * t_len, d)
    h = _ffn(xs, g, up1, down1, 0, 1)
    qa, ka, va, qb, kb, vb, kb32_s, vb32_s, ka32, va32 = _qkv(h, g, win, "sample")
    pos_q = past + np.arange(t_len)
    n_keys_a = -(-(l_a + t_len) // 128) * 128
    pos_k = np.concatenate([past - l_a + np.arange(l_a), pos_q, np.zeros(n_keys_a - l_a - t_len, np.int64)])
    pad = np.arange(n_keys_a) >= l_a + t_len
    band_bias_s = _band_bias(rel_bias[l], pos_q, pos_k, extra_invalid=pad | (pos_k < 0))
    ya, ak_s, av_s = _band_sample(qa, ka, va, ka32, va32, cache_a_k[l].reshape(n_s, l_a, A_WIDTH),
                                  cache_a_v[l].reshape(n_s, l_a, A_WIDTH), band_bias_s, t_len)
    bias_c = _alibi(pos_q, np.arange(past))
    pos_n = np.concatenate([pos_q, np.zeros(SAMPLE_KEY_PAD - t_len, np.int64)])
    bias_n = _alibi(pos_q, pos_n, invalid=np.broadcast_to(np.arange(SAMPLE_KEY_PAD) >= t_len,
                                                          (t_len, SAMPLE_KEY_PAD)))
    cache_kt = cache_b_k[l].transpose(0, 2, 3, 4, 1).reshape(n_s, B_WIDTH, past)
    cache_v = cache_b_v[l].reshape(n_s, past * B_HEADS, B_VDIM)
    yb = _diff_sample(qb, kb, vb, cache_kt, cache_v, bias_c, bias_n, lq, sg, lam_init, t_len)
    mk_s = cache_mem_k[l].reshape(n_s, n_mem, d).astype(BF16)
    mv_s = cache_mem_v[l].reshape(n_s, n_mem, d).astype(BF16)
    h = _merge(h, ya, yb, mk_s, mv_s, g, wg, bg, wa, wb, wo, wq, wmo, t_len)
    y_s = _ffn(h, g, up2, down2, 7, 8).reshape(n_s, t_len, d)

    dh_m = d // M_HEADS
    return (y_p, y_s,
            ak_p.reshape(1, n_p, keep_a, A_HEADS, A_HEAD_DIM), av_p.reshape(1, n_p, keep_a, A_HEADS, A_HEAD_DIM),
            kb32.reshape(1, n_p, s_len, B_HEADS, 2, B_HEAD_DIM), vb32.reshape(1, n_p, s_len, B_HEADS, B_VDIM),
            mkv32[:, :d].reshape(1, n_p, n_mem, M_HEADS, dh_m), mkv32[:, d:].reshape(1, n_p, n_mem, M_HEADS, dh_m),
            ak_s.reshape(1, n_s, l_a, A_HEADS, A_HEAD_DIM), av_s.reshape(1, n_s, l_a, A_HEADS, A_HEAD_DIM),
            kb32_s.reshape(1, n_s, t_len, B_HEADS, 2, B_HEAD_DIM), vb32_s.reshape(1, n_s, t_len, B_HEADS, B_VDIM))
```

```python
import functools
import math

import jax
import jax.numpy as jnp
import numpy as np
from jax import lax
from jax.experimental import pallas as pl
from jax.experimental.pallas import tpu as pltpu

F32 = jnp.float32
BF16 = jnp.bfloat16

CHUNK = 64
A_HEADS = 8
A_HEAD_DIM = 64
A_WIDTH = A_HEADS * A_HEAD_DIM
A_PREV_CHUNKS = 8
A_WINDOW = (A_PREV_CHUNKS + 1) * CHUNK
A_REL_MAX = 2 * CHUNK
B_HEADS = 4
B_HEAD_DIM = 64
B_VDIM = 2 * B_HEAD_DIM
B_WIDTH = B_HEADS * B_VDIM
M_HEADS = 4
NORM_EPS = 1e-6
NEG_INF = -1e30
LOG2E = math.log2(math.e)

V7X_VMEM_BYTES = 64 * 1024 * 1024
VMEM_LIMIT = V7X_VMEM_BYTES - 8 * 1024 * 1024

ROW_TILE = 512
A_QTILE = 256
B_TILE = 256


def _params(*sem):
    return pltpu.CompilerParams(dimension_semantics=sem, vmem_limit_bytes=VMEM_LIMIT)


def _const_spec(shape):
    zeros = (0,) * len(shape)
    return pl.BlockSpec(shape, lambda *_: zeros, pipeline_mode=pl.Buffered(1))


def _row_tile(rows):
    return max(t for t in range(16, min(ROW_TILE, rows) + 1, 16) if rows % t == 0)


def _rms(x, g):
    return x * lax.rsqrt(jnp.mean(x * x, axis=-1, keepdims=True) + NORM_EPS) * g


def _dot(a, b):
    return jnp.dot(a, b, preferred_element_type=F32)


def _dot_nt(a, b):
    return lax.dot_general(a, b, (((1,), (1,)), ((), ())), preferred_element_type=F32)


def _ffn_kernel(x_ref, g_ref, up_ref, down_ref, o_ref, *, pre, post, d_ff):
    x = x_ref[...]
    xn = _rms(x, g_ref[pre:pre + 1, :]).astype(BF16)
    gu = _dot(xn, up_ref[...])
    gate = gu[:, :d_ff]
    act = (gate * jax.nn.sigmoid(gate) * gu[:, d_ff:]).astype(BF16)
    y = _dot(act, down_ref[...])
    o_ref[...] = x + 0.5 * _rms(y, g_ref[post:post + 1, :])


def _ffn(x, g, up, down, pre, post):
    rows, d = x.shape
    d_ff = down.shape[0]
    tm = _row_tile(rows)
    return pl.pallas_call(
        functools.partial(_ffn_kernel, pre=pre, post=post, d_ff=d_ff),
        grid=(rows // tm,),
        in_specs=[pl.BlockSpec((tm, d), lambda i: (i, 0)),
                  _const_spec(g.shape), _const_spec(up.shape), _const_spec(down.shape)],
        out_specs=pl.BlockSpec((tm, d), lambda i: (i, 0)),
        out_shape=jax.ShapeDtypeStruct((rows, d), F32),
        compiler_params=_params("parallel"),
        name="ffn",
    )(x, g, up, down)


STAT_ROWS = 8


def _group_norm_max(x16, grp_ref):
    x = x16.astype(F32)
    sq = _dot((x * x).astype(BF16), grp_ref[...])
    blocks = [jnp.max(sq[r:r + B_TILE], axis=0, keepdims=True) for r in range(0, x.shape[0], B_TILE)]
    return jnp.concatenate(blocks + [jnp.zeros((STAT_ROWS - len(blocks), sq.shape[1]), F32)], axis=0)


def _qkv_kernel(h_ref, g_ref, w_ref, *refs, mode):
    u = _rms(h_ref[...], g_ref[2:3, :]).astype(BF16)
    y = _dot(u, w_ref[...])
    w = A_WIDTH
    qa, ka, va, qb, kb, vb = (y[:, j * w:(j + 1) * w] for j in range(6))
    if mode == "tail":
        refs[0][...] = ka
        refs[1][...] = va
        return
    out = refs[1:] if mode == "prompt" else refs
    scale = A_HEAD_DIM ** -0.5
    qb16 = (qb * (scale * LOG2E)).astype(BF16)
    kb16 = kb.astype(BF16)
    out[0][...] = (qa * scale).astype(BF16)
    out[1][...] = ka.astype(BF16)
    out[2][...] = va.astype(BF16)
    out[3][...] = qb16
    out[4][...] = kb16
    out[5][...] = vb.astype(BF16)
    if mode == "prompt":
        for hd in range(B_HEADS):
            out[7][pl.ds(hd, vb.shape[0], stride=B_HEADS), :] = vb[:, hd * B_VDIM:(hd + 1) * B_VDIM]
        out[6][0] = kb.T
        out[8][...] = _group_norm_max(qb16, refs[0])
        out[9][...] = _group_norm_max(kb16, refs[0])
    else:
        out[6][...] = kb
        out[7][...] = vb
        out[8][...] = ka
        out[9][...] = va


def _qkv(h, g, w_in, mode, s_len=None):
    assert A_WIDTH == B_WIDTH and A_HEAD_DIM == B_HEAD_DIM
    rows, d = h.shape
    tm = _row_tile(rows)
    spec = pl.BlockSpec((tm, A_WIDTH), lambda i: (i, 0))
    f32 = jax.ShapeDtypeStruct((rows, A_WIDTH), F32)
    b16 = jax.ShapeDtypeStruct((rows, A_WIDTH), BF16)
    in_specs = [pl.BlockSpec((tm, d), lambda i: (i, 0)), _const_spec(g.shape), _const_spec(w_in.shape)]
    args = [h, g, w_in]
    if mode == "tail":
        out_specs, out_shape = [spec] * 2, [f32] * 2
    elif mode == "sample":
        out_specs, out_shape = [spec] * 10, [b16] * 6 + [f32] * 4
    else:
        assert s_len % tm == 0 and tm % B_TILE == 0 and tm // B_TILE <= STAT_ROWS
        per = s_len // tm
        grp = (np.arange(A_WIDTH)[:, None] // B_HEAD_DIM == np.arange(128)[None, :]).astype(np.float32)
        in_specs.append(_const_spec(grp.shape))
        args.append(jnp.asarray(grp, BF16))
        stat_spec = pl.BlockSpec((STAT_ROWS, 128), lambda i: (i, 0))
        stat = jax.ShapeDtypeStruct((rows // tm * STAT_ROWS, 128), F32)
        kt_spec = pl.BlockSpec((1, A_WIDTH, tm), lambda i: (i // per, 0, i % per))
        kt = jax.ShapeDtypeStruct((rows // s_len, A_WIDTH, s_len), F32)
        vh_spec = pl.BlockSpec((tm * B_HEADS, B_VDIM), lambda i: (i, 0))
        vh = jax.ShapeDtypeStruct((rows * B_HEADS, B_VDIM), F32)
        out_specs = [spec] * 6 + [kt_spec, vh_spec, stat_spec, stat_spec]
        out_shape = [b16] * 6 + [kt, vh, stat, stat]
    return pl.pallas_call(
        functools.partial(_qkv_kernel, mode=mode),
        grid=(rows // tm,),
        in_specs=in_specs,
        out_specs=out_specs,
        out_shape=out_shape,
        compiler_params=_params("parallel"),
        name="qkv_" + mode,
    )(*args)


def _memkv_kernel(m_ref, g_ref, w_ref, kv_ref, kvb_ref):
    y = _dot(_rms(m_ref[...], g_ref[6:7, :]).astype(BF16), w_ref[...])
    kv_ref[...] = y
    kvb_ref[...] = y.astype(BF16)


def _memkv(mem, g, w_mkv):
    rows, d = mem.shape
    n = w_mkv.shape[1]
    return pl.pallas_call(
        _memkv_kernel,
        grid=(1,),
        in_specs=[_const_spec(mem.shape), _const_spec(g.shape), _const_spec(w_mkv.shape)],
        out_specs=[pl.BlockSpec((rows, n), lambda i: (0, 0))] * 2,
        out_shape=[jax.ShapeDtypeStruct((rows, n), F32), jax.ShapeDtypeStruct((rows, n), BF16)],
        compiler_params=_params("arbitrary"),
        name="memkv",
    )(mem, g, w_mkv)


def _band_kernel(q_ref, k0_ref, k1_ref, k2_ref, v0_ref, v1_ref, v2_ref, wtab_ref, o_ref, bias_scr):
    i = pl.program_id(1)
    q = q_ref[0]
    k = jnp.concatenate([k0_ref[0], k1_ref[0], k2_ref[0]], axis=0)
    v = jnp.concatenate([v0_ref[0], v1_ref[0], v2_ref[0]], axis=0)
    tq, tk = q.shape[0], k.shape[0]
    col = lax.broadcasted_iota(jnp.int32, (tq, tk), 1)

    @pl.when((pl.program_id(0) == 0) & (i == 0))
    def _build_bias():
        width = wtab_ref.shape[1]
        r_w = lax.broadcasted_iota(jnp.int32, (tq, width), 0)
        r = lax.broadcasted_iota(jnp.int32, (tq, tk), 0)
        back = r // CHUNK - col // CHUNK + A_PREV_CHUNKS
        in_band = (back >= 0) & (back <= A_PREV_CHUNKS)
        for h in range(A_HEADS):
            x = jnp.broadcast_to(wtab_ref[h:h + 1, :], (tq, width))
            for bit in range((tq - 1).bit_length()):
                x = jnp.where((r_w >> bit) & 1 == 1, pltpu.roll(x, 1 << bit, 1), x)
            bias_scr[h] = jnp.where(in_band, x[:, :tk], NEG_INF)

    valid = col + (i - 2) * tq >= 0
    outs = []
    for h in range(A_HEADS):
        sl = slice(h * A_HEAD_DIM, (h + 1) * A_HEAD_DIM)
        s = _dot_nt(q[:, sl], k[:, sl]) + bias_scr[h]
        s = jnp.where(valid, s, NEG_INF)
        e = jnp.exp(s - jnp.max(s, axis=-1, keepdims=True))
        l = jnp.sum(e, axis=-1, keepdims=True)
        outs.append(_dot(e.astype(BF16), v[:, sl]) / l)
    o_ref[0] = jnp.concatenate(outs, axis=1).astype(BF16)


def _band_prompt(qa, ka, va, wtab):
    n, s_len, w = qa.shape
    t = A_QTILE
    assert s_len % t == 0 and 2 * t == A_PREV_CHUNKS * CHUNK and wtab.shape[1] >= 4 * t
    qspec = pl.BlockSpec((1, t, w), lambda b, i: (b, i, 0))

    def kspec(back):
        return pl.BlockSpec((1, t, w), lambda b, i: (b, jnp.maximum(i - back, 0), 0))

    return pl.pallas_call(
        _band_kernel,
        grid=(n, s_len // t),
        in_specs=[qspec, kspec(2), kspec(1), kspec(0), kspec(2), kspec(1), kspec(0), _const_spec(wtab.shape)],
        out_specs=qspec,
        out_shape=jax.ShapeDtypeStruct((n, s_len, w), BF16),
        scratch_shapes=[pltpu.VMEM((A_HEADS, t, 3 * t), F32)],
        compiler_params=_params("arbitrary", "arbitrary"),
        name="band_prompt",
    )(qa, ka, ka, ka, va, va, va, wtab)


def _split_halves(q):
    lane = lax.broadcasted_iota(jnp.int32, q.shape, 1)
    zero = jnp.zeros_like(q)
    return jnp.concatenate([jnp.where(lane < B_HEAD_DIM, q, zero), jnp.where(lane >= B_HEAD_DIM, q, zero)], axis=0)


def _lambda(lam_ref, lam_init):
    lq = lam_ref[...]
    a = jnp.sum(lq[0:1, :] * lq[1:2, :], axis=-1, keepdims=True)
    b = jnp.sum(lq[2:3, :] * lq[3:4, :], axis=-1, keepdims=True)
    return jnp.exp(a) - jnp.exp(b) + lam_init


def _diff_finish(acc, l, lam, sg, lam_init):
    t = acc.shape[0] // 2
    o = acc[:t] / l[:t] - lam * (acc[t:] / l[t:])
    return _rms(o, sg) * (1.0 - lam_init)


DIFF_ACC_ROWS = B_VDIM + 16


def _diff_kernel(first_ref, q_ref, k_ref, v_ref, kbias_ref, dbias_ref, step_ref, lam_ref, sg_ref, o_ref,
                 qt_scr, vt_scr, s0_scr, s1_scr, m_scr, acc_scr, *, lam_init):
    i = pl.program_id(2)
    t = B_TILE
    n_tiles = k_ref.shape[1] // t
    first = first_ref[(pl.program_id(0) * B_HEADS + pl.program_id(1)) * n_tiles + i]

    def tile(j):
        return pl.ds(pl.multiple_of(j * t, t), t)

    @pl.when(i == 0)
    def _transpose_values():
        ones_row = (lax.broadcasted_iota(jnp.int32, (DIFF_ACC_ROWS - B_VDIM, t), 0) == 0).astype(BF16)

        def body(j, carry):
            vt_scr[0:B_VDIM, tile(j)] = v_ref[0, tile(j), :].astype(F32).T.astype(BF16)
            vt_scr[B_VDIM:DIFF_ACC_ROWS, tile(j)] = ones_row
            return carry

        lax.fori_loop(0, n_tiles, body, 0)

    qt = q_ref[0].astype(F32).T
    row = lax.broadcasted_iota(jnp.int32, qt.shape, 0)
    qt_scr[...] = jnp.concatenate([jnp.where(row < B_HEAD_DIM, qt, 0.0), jnp.where(row >= B_HEAD_DIM, qt, 0.0)],
                                  axis=1).astype(BF16)
    m_scr[...] = jnp.full(m_scr.shape, NEG_INF, F32)
    acc_scr[...] = jnp.zeros(acc_scr.shape, F32)
    step = step_ref[0, :, 0:1]

    def qk(j, s_scr):
        s_scr[...] = _dot(k_ref[0, tile(j), :], qt_scr[...])

    def softmax_pv(j, s_scr, bias_ref):
        s = s_scr[...] + bias_ref[0]
        m_old = m_scr[...]
        m_new = jnp.maximum(m_old, jnp.max(s, axis=0, keepdims=True))
        p = jnp.exp2(s - m_new).astype(BF16)
        acc_scr[...] = jnp.exp2(m_old - m_new) * acc_scr[...] + _dot(vt_scr[:, tile(j)], p)
        m_scr[...] = m_new + step

    n_before = i - first
    qk(first, s0_scr)

    def pair(j):
        qk(j + 1, s1_scr)
        softmax_pv(j, s0_scr, kbias_ref)
        qk(j + 2, s0_scr)
        softmax_pv(j + 1, s1_scr, kbias_ref)

    def quad(jj, carry):
        pair(first + 4 * jj)
        pair(first + 4 * jj + 2)
        return carry

    n_quads = n_before // 4
    lax.fori_loop(0, n_quads, quad, 0)

    @pl.when(n_before % 4 >= 2)
    def _pair():
        pair(first + 4 * n_quads)

    @pl.when(n_before % 2 == 1)
    def _odd():
        qk(i, s1_scr)
        softmax_pv(i - 1, s0_scr, kbias_ref)
        softmax_pv(i, s1_scr, dbias_ref)

    @pl.when(n_before % 2 == 0)
    def _even():
        softmax_pv(i, s0_scr, dbias_ref)

    acc = acc_scr[...]
    o = acc[0:B_VDIM, :] / acc[B_VDIM:B_VDIM + 1, :]
    o = o[:, :t] - _lambda(lam_ref, lam_init) * o[:, t:]
    o = o * lax.rsqrt(jnp.mean(o * o, axis=0, keepdims=True) + NORM_EPS) * sg_ref[...] * (1.0 - lam_init)
    o_ref[0] = o.T.astype(BF16)


F32_EXP2_UNDERFLOW = 150.0


def _first_key_tile(qstat, kstat, n, s_len, stat_tile):
    n_tiles = s_len // B_TILE
    per = stat_tile // B_TILE
    groups = 2 * B_HEADS

    def norms(stat):
        sq = stat.reshape(n, s_len // stat_tile, STAT_ROWS, 128)[:, :, :per, :groups]
        return jnp.sqrt(jnp.max(sq.reshape(n, n_tiles, B_HEADS, 2), axis=-1))

    q_max = norms(qstat)
    k_max = jnp.max(norms(kstat), axis=1, keepdims=True)
    sigma = jnp.asarray(_alibi_slopes() * LOG2E * B_TILE, F32)
    far = (F32_EXP2_UNDERFLOW + 8.0 + 2.06 * q_max * k_max) / sigma
    far = jnp.where(jnp.isfinite(far), jnp.minimum(far, float(n_tiles)), float(n_tiles))
    i = jnp.arange(n_tiles, dtype=jnp.int32)[None, :, None]
    first = jnp.maximum(i - 1 - jnp.floor(far).astype(jnp.int32), 0)
    return first.transpose(0, 2, 1).reshape(-1)


def _diff_prompt(first, qb, kb, vb, kbias, dbias, step, lam_qk, sg_col, lam_init):
    n, s_len, _ = qb.shape
    t = B_TILE
    assert s_len % t == 0 and t % CHUNK == 0
    qspec = pl.BlockSpec((1, t, B_VDIM), lambda b, h, i, first: (b, i, h))
    kvspec = pl.BlockSpec((1, s_len, B_VDIM), lambda b, h, i, first: (b, 0, h))
    hspec = pl.BlockSpec((1, t, 2 * t), lambda b, h, i, first: (h, 0, 0))
    return pl.pallas_call(
        functools.partial(_diff_kernel, lam_init=lam_init),
        grid_spec=pltpu.PrefetchScalarGridSpec(
            num_scalar_prefetch=1,
            grid=(n, B_HEADS, s_len // t),
            in_specs=[qspec, kvspec, kvspec, hspec, hspec,
                      pl.BlockSpec((1, 1, 128), lambda b, h, i, first: (h, 0, 0)),
                      _const_spec(lam_qk.shape), _const_spec(sg_col.shape)],
            out_specs=qspec,
            scratch_shapes=[pltpu.VMEM((B_VDIM, 2 * t), BF16), pltpu.VMEM((DIFF_ACC_ROWS, s_len), BF16),
                            pltpu.VMEM((t, 2 * t), F32), pltpu.VMEM((t, 2 * t), F32),
                            pltpu.VMEM((1, 2 * t), F32), pltpu.VMEM((DIFF_ACC_ROWS, 2 * t), F32)]),
        out_shape=jax.ShapeDtypeStruct(qb.shape, BF16),
        compiler_params=_params("parallel", "parallel", "arbitrary"),
        name="diff_prompt",
    )(first, qb, kb, vb, kbias, dbias, step, lam_qk, sg_col)


SAMPLE_KEY_PAD = 128


def _pad_rows(x, rows):
    return jnp.concatenate([x, jnp.zeros((rows - x.shape[0], x.shape[1]), x.dtype)], axis=0)


def _band_sample_kernel(q_ref, kn_ref, vn_ref, kn32_ref, vn32_ref, ck_ref, cv_ref, bias_ref,
                        o_ref, ko_ref, vo_ref):
    t = q_ref.shape[0]
    l_a = ck_ref.shape[1]
    ck = ck_ref[0]
    cv = cv_ref[0]
    ko_ref[0, 0:l_a - t, :] = ck[t:, :]
    ko_ref[0, l_a - t:l_a, :] = kn32_ref[...]
    vo_ref[0, 0:l_a - t, :] = cv[t:, :]
    vo_ref[0, l_a - t:l_a, :] = vn32_ref[...]
    n_keys = bias_ref.shape[2]
    k = jnp.concatenate([ck.astype(BF16), _pad_rows(kn_ref[...], n_keys - l_a)], axis=0)
    v = jnp.concatenate([cv.astype(BF16), _pad_rows(vn_ref[...], n_keys - l_a)], axis=0)
    q = q_ref[...]
    outs = []
    for h in range(A_HEADS):
        sl = slice(h * A_HEAD_DIM, (h + 1) * A_HEAD_DIM)
        s = _dot_nt(q[:, sl], k[:, sl]) + bias_ref[h]
        e = jnp.exp(s - jnp.max(s, axis=-1, keepdims=True))
        l = jnp.sum(e, axis=-1, keepdims=True)
        outs.append(_dot(e.astype(BF16), v[:, sl]) / l)
    o_ref[...] = jnp.concatenate(outs, axis=1).astype(BF16)


def _band_sample(qa, kn, vn, kn32, vn32, cache_k, cache_v, bias, t):
    n, l_a, w = cache_k.shape
    row = pl.BlockSpec((t, w), lambda b: (b, 0))
    cspec = pl.BlockSpec((1, l_a, w), lambda b: (b, 0, 0))
    return pl.pallas_call(
        _band_sample_kernel,
        grid=(n,),
        in_specs=[row, row, row, row, row, cspec, cspec, _const_spec(bias.shape)],
        out_specs=[row, cspec, cspec],
        out_shape=[jax.ShapeDtypeStruct(qa.shape, BF16),
                   jax.ShapeDtypeStruct(cache_k.shape, F32), jax.ShapeDtypeStruct(cache_v.shape, F32)],
        compiler_params=_params("parallel"),
        name="band_sample",
    )(qa, kn, vn, kn32, vn32, cache_k, cache_v, bias)


def _diff_sample_kernel(q_ref, kn_ref, vn_ref, ck_ref, cv_ref, bc_ref, bn_ref, lam_ref, sg_ref, o_ref,
                        *, lam_init):
    t = q_ref.shape[0]
    past = ck_ref.shape[2]
    lam = _lambda(lam_ref, lam_init)
    sg = sg_ref[...]
    outs = []
    for h in range(B_HEADS):
        sl = slice(h * B_VDIM, (h + 1) * B_VDIM)
        qq = _split_halves(q_ref[:, sl])
        kc = ck_ref[0, sl, :].astype(BF16)
        vc = cv_ref[0, pl.ds(h, past, stride=B_HEADS), :].astype(BF16)
        kn = _pad_rows(kn_ref[:, sl], SAMPLE_KEY_PAD)
        vn = _pad_rows(vn_ref[:, sl], SAMPLE_KEY_PAD)
        s_c = (_dot(qq, kc).reshape(2, t, -1) + bc_ref[h][None]).reshape(2 * t, -1)
        s_n = (_dot_nt(qq, kn).reshape(2, t, -1) + bn_ref[h][None]).reshape(2 * t, -1)
        m = jnp.maximum(jnp.max(s_c, axis=-1, keepdims=True), jnp.max(s_n, axis=-1, keepdims=True))
        e_c = jnp.exp2(s_c - m)
        e_n = jnp.exp2(s_n - m)
        l = jnp.sum(e_c, axis=-1, keepdims=True) + jnp.sum(e_n, axis=-1, keepdims=True)
        acc = _dot(e_c.astype(BF16), vc) + _dot(e_n.astype(BF16), vn)
        outs.append(_diff_finish(acc, l, lam, sg, lam_init))
    o_ref[...] = jnp.concatenate(outs, axis=1).astype(BF16)


def _diff_sample(qb, kn, vn, cache_kt, cache_v, bias_c, bias_n, lam_qk, sg, lam_init, t):
    n, w, past = cache_kt.shape
    row = pl.BlockSpec((t, w), lambda b: (b, 0))
    return pl.pallas_call(
        functools.partial(_diff_sample_kernel, lam_init=lam_init),
        grid=(n,),
        in_specs=[row, row, row, pl.BlockSpec((1, w, past), lambda b: (b, 0, 0)),
                  pl.BlockSpec((1, past * B_HEADS, B_VDIM), lambda b: (b, 0, 0)),
                  _const_spec(bias_c.shape), _const_spec(bias_n.shape),
                  _const_spec(lam_qk.shape), _const_spec(sg.shape)],
        out_specs=row,
        out_shape=jax.ShapeDtypeStruct(qb.shape, BF16),
        compiler_params=_params("parallel"),
        name="diff_sample",
    )(qb, kn, vn, cache_kt, cache_v, bias_c, bias_n, lam_qk, sg)


def _merge_kernel(h_ref, ya_ref, yb_ref, mk_ref, mv_ref, g_ref, wg_ref, bg_ref, wa_ref, wb_ref, wo_ref,
                  wq_ref, wmo_ref, o_ref, att_scr, *, streams, t):
    h = h_ref[...]
    d = h.shape[1]
    u = _rms(h, g_ref[2:3, :]).astype(BF16)
    gates = jax.nn.sigmoid(_dot(u, wg_ref[...]) + bg_ref[...])
    merged = gates[:, :d] * _dot(ya_ref[...], wa_ref[...]) + gates[:, d:] * _dot(yb_ref[...], wb_ref[...])
    h = h + _rms(_dot(merged.astype(BF16), wo_ref[...]), g_ref[3:4, :])

    xn = _rms(h, g_ref[4:5, :]).astype(BF16)
    dh = d // M_HEADS
    q = (_dot(xn, wq_ref[...]) * dh ** -0.5).astype(BF16)
    for b in range(streams):
        rows = slice(b * t, (b + 1) * t)
        for hd in range(M_HEADS):
            cols = slice(hd * dh, (hd + 1) * dh)
            s = _dot_nt(q[rows, cols], mk_ref[b, :, cols])
            e = jnp.exp(s - jnp.max(s, axis=-1, keepdims=True))
            l = jnp.sum(e, axis=-1, keepdims=True)
            att_scr[rows, cols] = (_dot(e.astype(BF16), mv_ref[b, :, cols]) / l).astype(BF16)
    o_ref[...] = h + _rms(_dot(att_scr[...], wmo_ref[...]), g_ref[5:6, :])


def _merge(h, ya, yb, mk, mv, g, wg, bg, wa, wb, wo, wq, wmo, t):
    rows, d = h.shape
    n_mem = mk.shape[1]
    if t >= ROW_TILE:
        tm, streams = ROW_TILE, 1
        assert t % tm == 0
        per = t // tm
        mspec = pl.BlockSpec((1, n_mem, d), lambda i: (i // per, 0, 0))
    else:
        streams = 8
        tm = streams * t
        mspec = pl.BlockSpec((streams, n_mem, d), lambda i: (i, 0, 0))
    assert rows % tm == 0
    t_in = tm // streams

    def row(width):
        return pl.BlockSpec((tm, width), lambda i: (i, 0))

    return pl.pallas_call(
        functools.partial(_merge_kernel, streams=streams, t=t_in),
        grid=(rows // tm,),
        in_specs=[row(d), row(ya.shape[1]), row(yb.shape[1]), mspec, mspec, _const_spec(g.shape),
                  _const_spec(wg.shape), _const_spec(bg.shape), _const_spec(wa.shape), _const_spec(wb.shape),
                  _const_spec(wo.shape), _const_spec(wq.shape), _const_spec(wmo.shape)],
        out_specs=row(d),
        out_shape=jax.ShapeDtypeStruct((rows, d), F32),
        scratch_shapes=[pltpu.VMEM((tm, d), BF16)],
        compiler_params=_params("parallel"),
        name="merge",
    )(h, ya, yb, mk, mv, g, wg, bg, wa, wb, wo, wq, wmo)


def _band_bias_sample(rel_bias, past, l_a, t_len, n_keys):
    pos_q = past + np.arange(t_len)
    pos_k = past - l_a + np.arange(n_keys)
    cq = pos_q[:, None] // CHUNK
    ck = pos_k[None, :] // CHUNK
    ok = (ck <= cq) & (ck >= cq - A_PREV_CHUNKS) & (pos_k >= 0)[None, :] & (np.arange(n_keys) < l_a + t_len)[None, :]
    dist = l_a - (np.arange(n_keys + t_len - 1) - (t_len - 1))
    table = rel_bias.astype(F32)[:, np.clip(dist, -A_REL_MAX, A_REL_MAX) + A_REL_MAX]
    bias = jnp.stack([table[:, t_len - 1 - r:t_len - 1 - r + n_keys] for r in range(t_len)], axis=1)
    return jnp.where(ok[None], bias, NEG_INF)


def _alibi_slopes():
    return 2.0 ** (-8.0 * np.arange(1, B_HEADS + 1) / B_HEADS)


def _alibi(pos_q, pos_k, invalid=None):
    dist = np.abs(pos_q[:, None] - pos_k[None, :]).astype(np.float64)
    bias = -(_alibi_slopes() * LOG2E)[:, None, None] * dist
    if invalid is not None:
        bias = np.where(invalid[None], NEG_INF, bias)
    return jnp.asarray(bias, F32)


def _diff_prompt_tables():
    sigma = (_alibi_slopes() * LOG2E)[:, None, None]
    c = np.arange(B_TILE)[:, None].astype(np.float64)
    r = np.arange(B_TILE)[None, :].astype(np.float64)
    kbias = np.broadcast_to(sigma * c, (B_HEADS, B_TILE, B_TILE))
    dbias = np.where(c <= r, sigma * c, sigma * (2 * r - c))
    dbias = np.where((c // CHUNK) > (r // CHUNK), NEG_INF, dbias)
    both = lambda x: jnp.asarray(np.concatenate([x, x], axis=2), F32)
    step = jnp.asarray(np.broadcast_to(-sigma * B_TILE, (B_HEADS, 1, 128)), F32)
    return both(kbias), both(dbias), step


def kernel(x_prompt, x_sample, cache_a_k, cache_a_v, cache_b_k, cache_b_v, cache_mem_k, cache_mem_v, mem_prompt,
           w_in, w_gate, b_gate, rel_bias, lam_qk, subln_g, w_br_a, w_br_b, w_out, w_mq, w_mkv, w_mo, norm_g,
           ffn1_up, ffn1_down, ffn2_up, ffn2_down):
    depth = w_in.shape[0]
    assert depth == 1
    n_p, s_len, d = x_prompt.shape
    n_s, t_len, _ = x_sample.shape
    past = cache_b_k.shape[2]
    l_a = cache_a_k.shape[2]
    keep_a = min(A_WINDOW, s_len)
    n_mem = mem_prompt.shape[1]
    l = 0
    lam_init = 0.8 - 0.6 * math.exp(-0.3 * l)

    g = norm_g[l]
    bf = lambda w: w[l].astype(BF16)
    up1, down1, up2, down2 = bf(ffn1_up), bf(ffn1_down), bf(ffn2_up), bf(ffn2_down)
    win, wg, wa, wb, wo, wq, wmkv, wmo = (bf(w_in), bf(w_gate), bf(w_br_a), bf(w_br_b), bf(w_out), bf(w_mq),
                                          bf(w_mkv), bf(w_mo))
    bg = b_gate[l][None, :]
    lq = lam_qk[l]
    sg = subln_g[l][None, :]

    width = 4 * A_QTILE
    cr = (np.arange(width) + A_QTILE) % width - A_QTILE
    band_wtab = rel_bias[l].astype(F32)[:, np.clip(2 * A_QTILE - cr, -A_REL_MAX, A_REL_MAX) + A_REL_MAX]
    kbias, dbias, step = _diff_prompt_tables()

    xp = x_prompt.reshape(n_p * s_len, d)
    h = _ffn(xp, g, up1, down1, 0, 1)
    qa, ka, va, qb, kb, vb, kbt32, vb32, qstat, kstat = _qkv(h, g, win, "prompt", s_len)
    tail = h.reshape(n_p, s_len, d)[:, s_len - keep_a:].reshape(n_p * keep_a, d)
    ak_p, av_p = _qkv(tail, g, win, "tail")
    as3 = lambda a: a.reshape(n_p, s_len, -1)
    ya = _band_prompt(as3(qa), as3(ka), as3(va), band_wtab)
    first = _first_key_tile(qstat, kstat, n_p, s_len, _row_tile(n_p * s_len))
    yb = _diff_prompt(first, as3(qb), as3(kb), as3(vb), kbias, dbias, step, lq, subln_g[l][:, None], lam_init)
    kb32 = kbt32.reshape(n_p, B_HEADS, 2, B_HEAD_DIM, s_len).transpose(0, 4, 1, 2, 3)
    mkv32, mkv16 = _memkv(mem_prompt.reshape(n_p * n_mem, d), g, wmkv)
    mk16 = mkv16[:, :d].reshape(n_p, n_mem, d)
    mv16 = mkv16[:, d:].reshape(n_p, n_mem, d)
    h = _merge(h, ya.reshape(-1, A_WIDTH), yb.reshape(-1, B_WIDTH), mk16, mv16, g, wg, bg, wa, wb, wo, wq, wmo,
               s_len)
    y_p = _ffn(h, g, up2, down2, 7, 8).reshape(n_p, s_len, d)

    assert past // CHUNK == (past + t_len - 1) // CHUNK, "sample queries must share one chunk"
    xs = x_sample.reshape(n_s * t_len, d)
    h = _ffn(xs, g, up1, down1, 0, 1)
    qa, ka, va, qb, kb, vb, kb32_s, vb32_s, ka32, va32 = _qkv(h, g, win, "sample")
    pos_q = past + np.arange(t_len)
    n_keys_a = -(-(l_a + t_len) // 128) * 128
    band_bias_s = _band_bias_sample(rel_bias[l], past, l_a, t_len, n_keys_a)
    ya, ak_s, av_s = _band_sample(qa, ka, va, ka32, va32, cache_a_k[l].reshape(n_s, l_a, A_WIDTH),
                                  cache_a_v[l].reshape(n_s, l_a, A_WIDTH), band_bias_s, t_len)
    bias_c = _alibi(pos_q, np.arange(past))
    pos_n = np.concatenate([pos_q, np.zeros(SAMPLE_KEY_PAD - t_len, np.int64)])
    bias_n = _alibi(pos_q, pos_n, invalid=np.broadcast_to(np.arange(SAMPLE_KEY_PAD) >= t_len,
                                                          (t_len, SAMPLE_KEY_PAD)))
    cache_kt = cache_b_k[l].transpose(0, 2, 3, 4, 1).reshape(n_s, B_WIDTH, past)
    cache_v = cache_b_v[l].reshape(n_s, past * B_HEADS, B_VDIM)
    yb = _diff_sample(qb, kb, vb, cache_kt, cache_v, bias_c, bias_n, lq, sg, lam_init, t_len)
    mk_s = cache_mem_k[l].reshape(n_s, n_mem, d).astype(BF16)
    mv_s = cache_mem_v[l].reshape(n_s, n_mem, d).astype(BF16)
    h = _merge(h, ya, yb, mk_s, mv_s, g, wg, bg, wa, wb, wo, wq, wmo, t_len)
    y_s = _ffn(h, g, up2, down2, 7, 8).reshape(n_s, t_len, d)

    dh_m = d // M_HEADS
    return (y_p, y_s,
            ak_p.reshape(1, n_p, keep_a, A_HEADS, A_HEAD_DIM), av_p.reshape(1, n_p, keep_a, A_HEADS, A_HEAD_DIM),
            kb32.reshape(1, n_p, s_len, B_HEADS, 2, B_HEAD_DIM), vb32.reshape(1, n_p, s_len, B_HEADS, B_VDIM),
            mkv32[:, :d].reshape(1, n_p, n_mem, M_HEADS, dh_m), mkv32[:, d:].reshape(1, n_p, n_mem, M_HEADS, dh_m),
            ak_s.reshape(1, n_s, l_a, A_HEADS, A_HEAD_DIM), av_s.reshape(1, n_s, l_a, A_HEADS, A_HEAD_DIM),
            kb32_s.reshape(1, n_s, t_len, B_HEADS, 2, B_HEAD_DIM), vb32_s.reshape(1, n_s, t_len, B_HEADS, B_VDIM))
```

```python
import functools
import math

import jax
import jax.numpy as jnp
import numpy as np
from jax import lax
from jax.experimental import pallas as pl
from jax.experimental.pallas import tpu as pltpu

F32 = jnp.float32
BF16 = jnp.bfloat16

CHUNK = 64
A_HEADS = 8
A_HEAD_DIM = 64
A_WIDTH = A_HEADS * A_HEAD_DIM
A_PREV_CHUNKS = 8
A_WINDOW = (A_PREV_CHUNKS + 1) * CHUNK
A_REL_MAX = 2 * CHUNK
B_HEADS = 4
B_HEAD_DIM = 64
B_VDIM = 2 * B_HEAD_DIM
B_WIDTH = B_HEADS * B_VDIM
M_HEADS = 4
NORM_EPS = 1e-6
NEG_INF = -1e30
LOG2E = math.log2(math.e)

V7X_VMEM_BYTES = 64 * 1024 * 1024
VMEM_LIMIT = V7X_VMEM_BYTES - 8 * 1024 * 1024

ROW_TILE = 512
A_QTILE = 256
B_TILE = 256


def _params(*sem):
    return pltpu.CompilerParams(dimension_semantics=sem, vmem_limit_bytes=VMEM_LIMIT)


def _const_spec(shape):
    zeros = (0,) * len(shape)
    return pl.BlockSpec(shape, lambda *_: zeros, pipeline_mode=pl.Buffered(1))


def _row_tile(rows):
    return max(t for t in range(16, min(ROW_TILE, rows) + 1, 16) if rows % t == 0)


def _rms(x, g):
    return x * lax.rsqrt(jnp.mean(x * x, axis=-1, keepdims=True) + NORM_EPS) * g


def _dot(a, b):
    return jnp.dot(a, b, preferred_element_type=F32)


def _dot_nt(a, b):
    return lax.dot_general(a, b, (((1,), (1,)), ((), ())), preferred_element_type=F32)


def _ffn_kernel(x_ref, g_ref, up_ref, down_ref, o_ref, *, pre, post, d_ff):
    x = x_ref[...]
    xn = _rms(x, g_ref[pre:pre + 1, :]).astype(BF16)
    gu = _dot(xn, up_ref[...])
    gate = gu[:, :d_ff]
    act = (gate * jax.nn.sigmoid(gate) * gu[:, d_ff:]).astype(BF16)
    y = _dot(act, down_ref[...])
    o_ref[...] = x + 0.5 * _rms(y, g_ref[post:post + 1, :])


def _ffn(x, g, up, down, pre, post):
    rows, d = x.shape
    d_ff = down.shape[0]
    tm = _row_tile(rows)
    return pl.pallas_call(
        functools.partial(_ffn_kernel, pre=pre, post=post, d_ff=d_ff),
        grid=(rows // tm,),
        in_specs=[pl.BlockSpec((tm, d), lambda i: (i, 0)),
                  _const_spec(g.shape), _const_spec(up.shape), _const_spec(down.shape)],
        out_specs=pl.BlockSpec((tm, d), lambda i: (i, 0)),
        out_shape=jax.ShapeDtypeStruct((rows, d), F32),
        compiler_params=_params("parallel"),
        name="ffn",
    )(x, g, up, down)


STAT_ROWS = 8


def _group_norm_max(x16, grp_ref):
    x = x16.astype(F32)
    sq = _dot((x * x).astype(BF16), grp_ref[...])
    blocks = [jnp.max(sq[r:r + B_TILE], axis=0, keepdims=True) for r in range(0, x.shape[0], B_TILE)]
    return jnp.concatenate(blocks + [jnp.zeros((STAT_ROWS - len(blocks), sq.shape[1]), F32)], axis=0)


def _qkv_kernel(h_ref, g_ref, w_ref, *refs, mode):
    u = _rms(h_ref[...], g_ref[2:3, :]).astype(BF16)
    y = _dot(u, w_ref[...])
    w = A_WIDTH
    qa, ka, va, qb, kb, vb = (y[:, j * w:(j + 1) * w] for j in range(6))
    if mode == "tail":
        refs[0][...] = ka
        refs[1][...] = va
        return
    out = refs[1:] if mode == "prompt" else refs
    scale = A_HEAD_DIM ** -0.5
    qb16 = (qb * (scale * LOG2E)).astype(BF16)
    kb16 = kb.astype(BF16)
    out[0][...] = (qa * (scale * LOG2E)).astype(BF16)
    out[1][...] = ka.astype(BF16)
    if mode == "prompt":
        out[2][0] = va.T.astype(BF16)
    else:
        out[2][...] = va.astype(BF16)
    out[3][...] = qb16
    out[4][...] = kb16
    out[5][...] = vb.astype(BF16)
    if mode == "prompt":
        for hd in range(B_HEADS):
            out[7][pl.ds(hd, vb.shape[0], stride=B_HEADS), :] = vb[:, hd * B_VDIM:(hd + 1) * B_VDIM]
        out[6][0] = kb.T
        out[8][...] = _group_norm_max(qb16, refs[0])
        out[9][...] = _group_norm_max(kb16, refs[0])
    else:
        out[6][...] = kb
        out[7][...] = vb
        out[8][...] = ka
        out[9][...] = va


def _qkv(h, g, w_in, mode, s_len=None):
    assert A_WIDTH == B_WIDTH and A_HEAD_DIM == B_HEAD_DIM
    rows, d = h.shape
    tm = _row_tile(rows)
    spec = pl.BlockSpec((tm, A_WIDTH), lambda i: (i, 0))
    f32 = jax.ShapeDtypeStruct((rows, A_WIDTH), F32)
    b16 = jax.ShapeDtypeStruct((rows, A_WIDTH), BF16)
    in_specs = [pl.BlockSpec((tm, d), lambda i: (i, 0)), _const_spec(g.shape), _const_spec(w_in.shape)]
    args = [h, g, w_in]
    if mode == "tail":
        out_specs, out_shape = [spec] * 2, [f32] * 2
    elif mode == "sample":
        out_specs, out_shape = [spec] * 10, [b16] * 6 + [f32] * 4
    else:
        assert s_len % tm == 0 and tm % B_TILE == 0 and tm // B_TILE <= STAT_ROWS
        per = s_len // tm
        grp = (np.arange(A_WIDTH)[:, None] // B_HEAD_DIM == np.arange(128)[None, :]).astype(np.float32)
        in_specs.append(_const_spec(grp.shape))
        args.append(jnp.asarray(grp, BF16))
        stat_spec = pl.BlockSpec((STAT_ROWS, 128), lambda i: (i, 0))
        stat = jax.ShapeDtypeStruct((rows // tm * STAT_ROWS, 128), F32)
        kt_spec = pl.BlockSpec((1, A_WIDTH, tm), lambda i: (i // per, 0, i % per))
        kt = jax.ShapeDtypeStruct((rows // s_len, A_WIDTH, s_len), F32)
        vh_spec = pl.BlockSpec((tm * B_HEADS, B_VDIM), lambda i: (i, 0))
        vh = jax.ShapeDtypeStruct((rows * B_HEADS, B_VDIM), F32)
        vat = jax.ShapeDtypeStruct((rows // s_len, A_WIDTH, s_len), BF16)
        out_specs = [spec, spec, kt_spec, spec, spec, spec, kt_spec, vh_spec, stat_spec, stat_spec]
        out_shape = [b16, b16, vat, b16, b16, b16, kt, vh, stat, stat]
    return pl.pallas_call(
        functools.partial(_qkv_kernel, mode=mode),
        grid=(rows // tm,),
        in_specs=in_specs,
        out_specs=out_specs,
        out_shape=out_shape,
        compiler_params=_params("parallel"),
        name="qkv_" + mode,
    )(*args)


def _memkv_kernel(m_ref, g_ref, w_ref, kv_ref, kvb_ref):
    y = _dot(_rms(m_ref[...], g_ref[6:7, :]).astype(BF16), w_ref[...])
    kv_ref[...] = y
    kvb_ref[...] = y.astype(BF16)


def _memkv(mem, g, w_mkv):
    rows, d = mem.shape
    n = w_mkv.shape[1]
    return pl.pallas_call(
        _memkv_kernel,
        grid=(1,),
        in_specs=[_const_spec(mem.shape), _const_spec(g.shape), _const_spec(w_mkv.shape)],
        out_specs=[pl.BlockSpec((rows, n), lambda i: (0, 0))] * 2,
        out_shape=[jax.ShapeDtypeStruct((rows, n), F32), jax.ShapeDtypeStruct((rows, n), BF16)],
        compiler_params=_params("arbitrary"),
        name="memkv",
    )(mem, g, w_mkv)


BAND_PAIR = 2 * A_HEAD_DIM
BAND_ACC_ROWS = BAND_PAIR + 16


def _band_kernel(q_ref, k0_ref, k1_ref, k2_ref, v0_ref, v1_ref, v2_ref, wtab_ref, o_ref, bias_scr):
    i = pl.program_id(1)
    tq = q_ref.shape[1]
    tk = 3 * tq

    @pl.when((pl.program_id(0) == 0) & (i == 0))
    def _build_bias():
        width = wtab_ref.shape[1]
        r_w = lax.broadcasted_iota(jnp.int32, (tq, width), 0)
        r = lax.broadcasted_iota(jnp.int32, (tq, tk), 0)
        col = lax.broadcasted_iota(jnp.int32, (tq, tk), 1)
        back = r // CHUNK - col // CHUNK + A_PREV_CHUNKS
        in_band = (back >= 0) & (back <= A_PREV_CHUNKS)
        for h in range(A_HEADS):
            x = jnp.broadcast_to(wtab_ref[h:h + 1, :], (tq, width))
            for bit in range((tq - 1).bit_length()):
                x = jnp.where((r_w >> bit) & 1 == 1, pltpu.roll(x, 1 << bit, 1), x)
            bias_scr[h // 2, :, (h % 2) * tq:(h % 2 + 1) * tq] = jnp.where(in_band, x[:, :tk], NEG_INF).T

    def attend(mask_missing):
        qt = q_ref[0].astype(F32).T
        row = lax.broadcasted_iota(jnp.int32, (BAND_PAIR, tq), 0)
        ones_rows = (lax.broadcasted_iota(jnp.int32, (BAND_ACC_ROWS - BAND_PAIR, tk), 0) == 0).astype(BF16)
        for p in range(A_HEADS // 2):
            cols = slice(p * BAND_PAIR, (p + 1) * BAND_PAIR)
            k_p = jnp.concatenate([k0_ref[0, :, cols], k1_ref[0, :, cols], k2_ref[0, :, cols]], axis=0)
            vt_p = jnp.concatenate([v0_ref[0, cols, :], v1_ref[0, cols, :], v2_ref[0, cols, :]], axis=1)
            qt_p = qt[cols, :]
            qq = jnp.concatenate([jnp.where(row < A_HEAD_DIM, qt_p, 0.0), jnp.where(row >= A_HEAD_DIM, qt_p, 0.0)],
                                 axis=1).astype(BF16)
            s = _dot(k_p, qq) + bias_scr[p]
            if mask_missing:
                key = lax.broadcasted_iota(jnp.int32, s.shape, 0)
                s = jnp.where(key + (i - 2) * tq >= 0, s, NEG_INF)
            e = jnp.exp2(s - jnp.max(s, axis=0, keepdims=True)).astype(BF16)
            acc = _dot(jnp.concatenate([vt_p, ones_rows], axis=0), e)
            o = acc[0:BAND_PAIR] * (1.0 / acc[BAND_PAIR:BAND_PAIR + 1])
            o = jnp.concatenate([o[0:A_HEAD_DIM, 0:tq], o[A_HEAD_DIM:BAND_PAIR, tq:2 * tq]], axis=0)
            o_ref[0, :, cols] = o.T.astype(BF16)

    pl.when(i < 2)(functools.partial(attend, True))
    pl.when(i >= 2)(functools.partial(attend, False))


def _band_prompt(qa, ka, vat, wtab):
    n, s_len, w = qa.shape
    t = A_QTILE
    assert s_len % t == 0 and 2 * t == A_PREV_CHUNKS * CHUNK and wtab.shape[1] >= 4 * t
    qspec = pl.BlockSpec((1, t, w), lambda b, i: (b, i, 0))

    def kspec(back):
        return pl.BlockSpec((1, t, w), lambda b, i: (b, jnp.maximum(i - back, 0), 0))

    def vspec(back):
        return pl.BlockSpec((1, w, t), lambda b, i: (b, 0, jnp.maximum(i - back, 0)))

    return pl.pallas_call(
        _band_kernel,
        grid=(n, s_len // t),
        in_specs=[qspec, kspec(2), kspec(1), kspec(0), vspec(2), vspec(1), vspec(0), _const_spec(wtab.shape)],
        out_specs=qspec,
        out_shape=jax.ShapeDtypeStruct((n, s_len, w), BF16),
        scratch_shapes=[pltpu.VMEM((A_HEADS // 2, 3 * t, 2 * t), F32)],
        compiler_params=_params("arbitrary", "arbitrary"),
        name="band_prompt",
    )(qa, ka, ka, ka, vat, vat, vat, wtab)


def _split_halves(q):
    lane = lax.broadcasted_iota(jnp.int32, q.shape, 1)
    zero = jnp.zeros_like(q)
    return jnp.concatenate([jnp.where(lane < B_HEAD_DIM, q, zero), jnp.where(lane >= B_HEAD_DIM, q, zero)], axis=0)


def _lambda(lam_ref, lam_init):
    lq = lam_ref[...]
    a = jnp.sum(lq[0:1, :] * lq[1:2, :], axis=-1, keepdims=True)
    b = jnp.sum(lq[2:3, :] * lq[3:4, :], axis=-1, keepdims=True)
    return jnp.exp(a) - jnp.exp(b) + lam_init


def _diff_finish(acc, l, lam, sg, lam_init):
    t = acc.shape[0] // 2
    o = acc[:t] / l[:t] - lam * (acc[t:] / l[t:])
    return _rms(o, sg) * (1.0 - lam_init)


DIFF_ACC_ROWS = B_VDIM + 16


def _diff_kernel(first_ref, q_ref, k_ref, v_ref, kbias_ref, dbias_ref, step_ref, lam_ref, sg_ref, o_ref,
                 qt_scr, vt_scr, s0_scr, s1_scr, m_scr, acc_scr, *, lam_init):
    i = pl.program_id(2)
    t = B_TILE
    n_tiles = k_ref.shape[1] // t
    first = first_ref[(pl.program_id(0) * B_HEADS + pl.program_id(1)) * n_tiles + i]

    def tile(j):
        return pl.ds(pl.multiple_of(j * t, t), t)

    @pl.when(i == 0)
    def _transpose_values():
        ones_row = (lax.broadcasted_iota(jnp.int32, (DIFF_ACC_ROWS - B_VDIM, t), 0) == 0).astype(BF16)

        def body(j, carry):
            vt_scr[0:B_VDIM, tile(j)] = v_ref[0, tile(j), :].astype(F32).T.astype(BF16)
            vt_scr[B_VDIM:DIFF_ACC_ROWS, tile(j)] = ones_row
            return carry

        lax.fori_loop(0, n_tiles, body, 0)

    qt = q_ref[0].astype(F32).T
    row = lax.broadcasted_iota(jnp.int32, qt.shape, 0)
    qt_scr[...] = jnp.concatenate([jnp.where(row < B_HEAD_DIM, qt, 0.0), jnp.where(row >= B_HEAD_DIM, qt, 0.0)],
                                  axis=1).astype(BF16)
    m_scr[...] = jnp.full(m_scr.shape, NEG_INF, F32)
    acc_scr[...] = jnp.zeros(acc_scr.shape, F32)
    step = step_ref[0, :, 0:1]

    def qk(j, s_scr):
        s_scr[...] = _dot(k_ref[0, tile(j), :], qt_scr[...])

    def softmax_pv(j, s_scr, bias_ref):
        s = s_scr[...] + bias_ref[0]
        m_old = m_scr[...]
        m_new = jnp.maximum(m_old, jnp.max(s, axis=0, keepdims=True))
        p = jnp.exp2(s - m_new).astype(BF16)
        acc_scr[...] = jnp.exp2(m_old - m_new) * acc_scr[...] + _dot(vt_scr[:, tile(j)], p)
        m_scr[...] = m_new + step

    n_before = i - first
    qk(first, s0_scr)

    def pair(j):
        qk(j + 1, s1_scr)
        softmax_pv(j, s0_scr, kbias_ref)
        qk(j + 2, s0_scr)
        softmax_pv(j + 1, s1_scr, kbias_ref)

    def quad(jj, carry):
        pair(first + 4 * jj)
        pair(first + 4 * jj + 2)
        return carry

    n_quads = n_before // 4
    lax.fori_loop(0, n_quads, quad, 0)

    @pl.when(n_before % 4 >= 2)
    def _pair():
        pair(first + 4 * n_quads)

    @pl.when(n_before % 2 == 1)
    def _odd():
        qk(i, s1_scr)
        softmax_pv(i - 1, s0_scr, kbias_ref)
        softmax_pv(i, s1_scr, dbias_ref)

    @pl.when(n_before % 2 == 0)
    def _even():
        softmax_pv(i, s0_scr, dbias_ref)

    acc = acc_scr[...]
    o = acc[0:B_VDIM, :] * (1.0 / acc[B_VDIM:B_VDIM + 1, :])
    o = o[:, :t] - _lambda(lam_ref, lam_init) * o[:, t:]
    o = o * lax.rsqrt(jnp.mean(o * o, axis=0, keepdims=True) + NORM_EPS) * sg_ref[...] * (1.0 - lam_init)
    o_ref[0] = o.T.astype(BF16)


F32_EXP2_UNDERFLOW = 150.0


def _first_key_tile(qstat, kstat, n, s_len, stat_tile):
    n_tiles = s_len // B_TILE
    per = stat_tile // B_TILE
    groups = 2 * B_HEADS

    def norms(stat):
        sq = stat.reshape(n, s_len // stat_tile, STAT_ROWS, 128)[:, :, :per, :groups]
        return jnp.sqrt(jnp.max(sq.reshape(n, n_tiles, B_HEADS, 2), axis=-1))

    q_max = norms(qstat)
    k_max = jnp.max(norms(kstat), axis=1, keepdims=True)
    sigma = jnp.asarray(_alibi_slopes() * LOG2E * B_TILE, F32)
    far = (F32_EXP2_UNDERFLOW + 8.0 + 2.06 * q_max * k_max) / sigma
    far = jnp.where(jnp.isfinite(far), jnp.minimum(far, float(n_tiles)), float(n_tiles))
    i = jnp.arange(n_tiles, dtype=jnp.int32)[None, :, None]
    first = jnp.maximum(i - 1 - jnp.floor(far).astype(jnp.int32), 0)
    return first.transpose(0, 2, 1).reshape(-1)


def _diff_prompt(first, qb, kb, vb, kbias, dbias, step, lam_qk, sg_col, lam_init):
    n, s_len, _ = qb.shape
    t = B_TILE
    assert s_len % t == 0 and t % CHUNK == 0
    qspec = pl.BlockSpec((1, t, B_VDIM), lambda b, h, i, first: (b, i, h))
    kvspec = pl.BlockSpec((1, s_len, B_VDIM), lambda b, h, i, first: (b, 0, h))
    hspec = pl.BlockSpec((1, t, 2 * t), lambda b, h, i, first: (h, 0, 0))
    return pl.pallas_call(
        functools.partial(_diff_kernel, lam_init=lam_init),
        grid_spec=pltpu.PrefetchScalarGridSpec(
            num_scalar_prefetch=1,
            grid=(n, B_HEADS, s_len // t),
            in_specs=[qspec, kvspec, kvspec, hspec, hspec,
                      pl.BlockSpec((1, 1, 128), lambda b, h, i, first: (h, 0, 0)),
                      _const_spec(lam_qk.shape), _const_spec(sg_col.shape)],
            out_specs=qspec,
            scratch_shapes=[pltpu.VMEM((B_VDIM, 2 * t), BF16), pltpu.VMEM((DIFF_ACC_ROWS, s_len), BF16),
                            pltpu.VMEM((t, 2 * t), F32), pltpu.VMEM((t, 2 * t), F32),
                            pltpu.VMEM((1, 2 * t), F32), pltpu.VMEM((DIFF_ACC_ROWS, 2 * t), F32)]),
        out_shape=jax.ShapeDtypeStruct(qb.shape, BF16),
        compiler_params=_params("parallel", "parallel", "arbitrary"),
        name="diff_prompt",
    )(first, qb, kb, vb, kbias, dbias, step, lam_qk, sg_col)


SAMPLE_KEY_PAD = 128


def _pad_rows(x, rows):
    return jnp.concatenate([x, jnp.zeros((rows - x.shape[0], x.shape[1]), x.dtype)], axis=0)


def _band_sample_kernel(q_ref, kn_ref, vn_ref, kn32_ref, vn32_ref, ck_ref, cv_ref, bias_ref,
                        o_ref, ko_ref, vo_ref):
    t = q_ref.shape[0]
    l_a = ck_ref.shape[1]
    ck = ck_ref[0]
    cv = cv_ref[0]
    ko_ref[0, 0:l_a - t, :] = ck[t:, :]
    ko_ref[0, l_a - t:l_a, :] = kn32_ref[...]
    vo_ref[0, 0:l_a - t, :] = cv[t:, :]
    vo_ref[0, l_a - t:l_a, :] = vn32_ref[...]
    n_keys = bias_ref.shape[2]
    k = jnp.concatenate([ck.astype(BF16), _pad_rows(kn_ref[...], n_keys - l_a)], axis=0)
    v = jnp.concatenate([cv.astype(BF16), _pad_rows(vn_ref[...], n_keys - l_a)], axis=0)
    q = q_ref[...]
    outs = []
    for h in range(A_HEADS):
        sl = slice(h * A_HEAD_DIM, (h + 1) * A_HEAD_DIM)
        s = _dot_nt(q[:, sl], k[:, sl]) + bias_ref[h]
        e = jnp.exp2(s - jnp.max(s, axis=-1, keepdims=True))
        l = jnp.sum(e, axis=-1, keepdims=True)
        outs.append(_dot(e.astype(BF16), v[:, sl]) / l)
    o_ref[...] = jnp.concatenate(outs, axis=1).astype(BF16)


def _band_sample(qa, kn, vn, kn32, vn32, cache_k, cache_v, bias, t):
    n, l_a, w = cache_k.shape
    row = pl.BlockSpec((t, w), lambda b: (b, 0))
    cspec = pl.BlockSpec((1, l_a, w), lambda b: (b, 0, 0))
    return pl.pallas_call(
        _band_sample_kernel,
        grid=(n,),
        in_specs=[row, row, row, row, row, cspec, cspec, _const_spec(bias.shape)],
        out_specs=[row, cspec, cspec],
        out_shape=[jax.ShapeDtypeStruct(qa.shape, BF16),
                   jax.ShapeDtypeStruct(cache_k.shape, F32), jax.ShapeDtypeStruct(cache_v.shape, F32)],
        compiler_params=_params("parallel"),
        name="band_sample",
    )(qa, kn, vn, kn32, vn32, cache_k, cache_v, bias)


def _diff_sample_kernel(q_ref, kn_ref, vn_ref, ck_ref, cv_ref, bc_ref, bn_ref, lam_ref, sg_ref, o_ref,
                        *, lam_init):
    t = q_ref.shape[0]
    past = ck_ref.shape[2]
    lam = _lambda(lam_ref, lam_init)
    sg = sg_ref[...]
    outs = []
    for h in range(B_HEADS):
        sl = slice(h * B_VDIM, (h + 1) * B_VDIM)
        qq = _split_halves(q_ref[:, sl])
        kc = ck_ref[0, sl, :].astype(BF16)
        vc = cv_ref[0, pl.ds(h, past, stride=B_HEADS), :].astype(BF16)
        kn = _pad_rows(kn_ref[:, sl], SAMPLE_KEY_PAD)
        vn = _pad_rows(vn_ref[:, sl], SAMPLE_KEY_PAD)
        s_c = (_dot(qq, kc).reshape(2, t, -1) + bc_ref[h][None]).reshape(2 * t, -1)
        s_n = (_dot_nt(qq, kn).reshape(2, t, -1) + bn_ref[h][None]).reshape(2 * t, -1)
        m = jnp.maximum(jnp.max(s_c, axis=-1, keepdims=True), jnp.max(s_n, axis=-1, keepdims=True))
        e_c = jnp.exp2(s_c - m)
        e_n = jnp.exp2(s_n - m)
        l = jnp.sum(e_c, axis=-1, keepdims=True) + jnp.sum(e_n, axis=-1, keepdims=True)
        acc = _dot(e_c.astype(BF16), vc) + _dot(e_n.astype(BF16), vn)
        outs.append(_diff_finish(acc, l, lam, sg, lam_init))
    o_ref[...] = jnp.concatenate(outs, axis=1).astype(BF16)


def _diff_sample(qb, kn, vn, cache_kt, cache_v, bias_c, bias_n, lam_qk, sg, lam_init, t):
    n, w, past = cache_kt.shape
    row = pl.BlockSpec((t, w), lambda b: (b, 0))
    return pl.pallas_call(
        functools.partial(_diff_sample_kernel, lam_init=lam_init),
        grid=(n,),
        in_specs=[row, row, row, pl.BlockSpec((1, w, past), lambda b: (b, 0, 0)),
                  pl.BlockSpec((1, past * B_HEADS, B_VDIM), lambda b: (b, 0, 0)),
                  _const_spec(bias_c.shape), _const_spec(bias_n.shape),
                  _const_spec(lam_qk.shape), _const_spec(sg.shape)],
        out_specs=row,
        out_shape=jax.ShapeDtypeStruct(qb.shape, BF16),
        compiler_params=_params("parallel"),
        name="diff_sample",
    )(qb, kn, vn, cache_kt, cache_v, bias_c, bias_n, lam_qk, sg)


def _merge_kernel(h_ref, ya_ref, yb_ref, mk_ref, mv_ref, g_ref, wg_ref, bg_ref, wa_ref, wb_ref, wo_ref,
                  wq_ref, wmo_ref, o_ref, att_scr, *, streams, t):
    h = h_ref[...]
    d = h.shape[1]
    u = _rms(h, g_ref[2:3, :]).astype(BF16)
    gates = jax.nn.sigmoid(_dot(u, wg_ref[...]) + bg_ref[...])
    merged = gates[:, :d] * _dot(ya_ref[...], wa_ref[...]) + gates[:, d:] * _dot(yb_ref[...], wb_ref[...])
    h = h + _rms(_dot(merged.astype(BF16), wo_ref[...]), g_ref[3:4, :])

    xn = _rms(h, g_ref[4:5, :]).astype(BF16)
    dh = d // M_HEADS
    q = (_dot(xn, wq_ref[...]) * dh ** -0.5).astype(BF16)
    for b in range(streams):
        rows = slice(b * t, (b + 1) * t)
        for hd in range(M_HEADS):
            cols = slice(hd * dh, (hd + 1) * dh)
            s = _dot_nt(q[rows, cols], mk_ref[b, :, cols])
            e = jnp.exp(s - jnp.max(s, axis=-1, keepdims=True))
            l = jnp.sum(e, axis=-1, keepdims=True)
            att_scr[rows, cols] = (_dot(e.astype(BF16), mv_ref[b, :, cols]) / l).astype(BF16)
    o_ref[...] = h + _rms(_dot(att_scr[...], wmo_ref[...]), g_ref[5:6, :])


def _merge(h, ya, yb, mk, mv, g, wg, bg, wa, wb, wo, wq, wmo, t):
    rows, d = h.shape
    n_mem = mk.shape[1]
    if t >= ROW_TILE:
        tm, streams = ROW_TILE, 1
        assert t % tm == 0
        per = t // tm
        mspec = pl.BlockSpec((1, n_mem, d), lambda i: (i // per, 0, 0))
    else:
        streams = 8
        tm = streams * t
        mspec = pl.BlockSpec((streams, n_mem, d), lambda i: (i, 0, 0))
    assert rows % tm == 0
    t_in = tm // streams

    def row(width):
        return pl.BlockSpec((tm, width), lambda i: (i, 0))

    return pl.pallas_call(
        functools.partial(_merge_kernel, streams=streams, t=t_in),
        grid=(rows // tm,),
        in_specs=[row(d), row(ya.shape[1]), row(yb.shape[1]), mspec, mspec, _const_spec(g.shape),
                  _const_spec(wg.shape), _const_spec(bg.shape), _const_spec(wa.shape), _const_spec(wb.shape),
                  _const_spec(wo.shape), _const_spec(wq.shape), _const_spec(wmo.shape)],
        out_specs=row(d),
        out_shape=jax.ShapeDtypeStruct((rows, d), F32),
        scratch_shapes=[pltpu.VMEM((tm, d), BF16)],
        compiler_params=_params("parallel"),
        name="merge",
    )(h, ya, yb, mk, mv, g, wg, bg, wa, wb, wo, wq, wmo)


def _band_bias_sample(rel_bias, past, l_a, t_len, n_keys):
    pos_q = past + np.arange(t_len)
    pos_k = past - l_a + np.arange(n_keys)
    cq = pos_q[:, None] // CHUNK
    ck = pos_k[None, :] // CHUNK
    ok = (ck <= cq) & (ck >= cq - A_PREV_CHUNKS) & (pos_k >= 0)[None, :] & (np.arange(n_keys) < l_a + t_len)[None, :]
    dist = l_a - (np.arange(n_keys + t_len - 1) - (t_len - 1))
    table = rel_bias.astype(F32)[:, np.clip(dist, -A_REL_MAX, A_REL_MAX) + A_REL_MAX] * LOG2E
    bias = jnp.stack([table[:, t_len - 1 - r:t_len - 1 - r + n_keys] for r in range(t_len)], axis=1)
    return jnp.where(ok[None], bias, NEG_INF)


def _alibi_slopes():
    return 2.0 ** (-8.0 * np.arange(1, B_HEADS + 1) / B_HEADS)


def _alibi(pos_q, pos_k, invalid=None):
    dist = np.abs(pos_q[:, None] - pos_k[None, :]).astype(np.float64)
    bias = -(_alibi_slopes() * LOG2E)[:, None, None] * dist
    if invalid is not None:
        bias = np.where(invalid[None], NEG_INF, bias)
    return jnp.asarray(bias, F32)


def _diff_prompt_tables():
    sigma = (_alibi_slopes() * LOG2E)[:, None, None]
    c = np.arange(B_TILE)[:, None].astype(np.float64)
    r = np.arange(B_TILE)[None, :].astype(np.float64)
    kbias = np.broadcast_to(sigma * c, (B_HEADS, B_TILE, B_TILE))
    dbias = np.where(c <= r, sigma * c, sigma * (2 * r - c))
    dbias = np.where((c // CHUNK) > (r // CHUNK), NEG_INF, dbias)
    both = lambda x: jnp.asarray(np.concatenate([x, x], axis=2), F32)
    step = jnp.asarray(np.broadcast_to(-sigma * B_TILE, (B_HEADS, 1, 128)), F32)
    return both(kbias), both(dbias), step


def kernel(x_prompt, x_sample, cache_a_k, cache_a_v, cache_b_k, cache_b_v, cache_mem_k, cache_mem_v, mem_prompt,
           w_in, w_gate, b_gate, rel_bias, lam_qk, subln_g, w_br_a, w_br_b, w_out, w_mq, w_mkv, w_mo, norm_g,
           ffn1_up, ffn1_down, ffn2_up, ffn2_down):
    depth = w_in.shape[0]
    assert depth == 1
    n_p, s_len, d = x_prompt.shape
    n_s, t_len, _ = x_sample.shape
    past = cache_b_k.shape[2]
    l_a = cache_a_k.shape[2]
    keep_a = min(A_WINDOW, s_len)
    n_mem = mem_prompt.shape[1]
    l = 0
    lam_init = 0.8 - 0.6 * math.exp(-0.3 * l)

    g = norm_g[l]
    bf = lambda w: w[l].astype(BF16)
    up1, down1, up2, down2 = bf(ffn1_up), bf(ffn1_down), bf(ffn2_up), bf(ffn2_down)
    win, wg, wa, wb, wo, wq, wmkv, wmo = (bf(w_in), bf(w_gate), bf(w_br_a), bf(w_br_b), bf(w_out), bf(w_mq),
                                          bf(w_mkv), bf(w_mo))
    bg = b_gate[l][None, :]
    lq = lam_qk[l]
    sg = subln_g[l][None, :]

    width = 4 * A_QTILE
    cr = (np.arange(width) + A_QTILE) % width - A_QTILE
    band_wtab = rel_bias[l].astype(F32)[:, np.clip(2 * A_QTILE - cr, -A_REL_MAX, A_REL_MAX) + A_REL_MAX] * LOG2E
    kbias, dbias, step = _diff_prompt_tables()

    xp = x_prompt.reshape(n_p * s_len, d)
    h = _ffn(xp, g, up1, down1, 0, 1)
    qa, ka, vat, qb, kb, vb, kbt32, vb32, qstat, kstat = _qkv(h, g, win, "prompt", s_len)
    tail = h.reshape(n_p, s_len, d)[:, s_len - keep_a:].reshape(n_p * keep_a, d)
    ak_p, av_p = _qkv(tail, g, win, "tail")
    as3 = lambda a: a.reshape(n_p, s_len, -1)
    ya = _band_prompt(as3(qa), as3(ka), vat, band_wtab)
    first = _first_key_tile(qstat, kstat, n_p, s_len, _row_tile(n_p * s_len))
    yb = _diff_prompt(first, as3(qb), as3(kb), as3(vb), kbias, dbias, step, lq, subln_g[l][:, None], lam_init)
    kb32 = kbt32.reshape(n_p, B_HEADS, 2, B_HEAD_DIM, s_len).transpose(0, 4, 1, 2, 3)
    mkv32, mkv16 = _memkv(mem_prompt.reshape(n_p * n_mem, d), g, wmkv)
    mk16 = mkv16[:, :d].reshape(n_p, n_mem, d)
    mv16 = mkv16[:, d:].reshape(n_p, n_mem, d)
    h = _merge(h, ya.reshape(-1, A_WIDTH), yb.reshape(-1, B_WIDTH), mk16, mv16, g, wg, bg, wa, wb, wo, wq, wmo,
               s_len)
    y_p = _ffn(h, g, up2, down2, 7, 8).reshape(n_p, s_len, d)

    assert past // CHUNK == (past + t_len - 1) // CHUNK, "sample queries must share one chunk"
    xs = x_sample.reshape(n_s * t_len, d)
    h = _ffn(xs, g, up1, down1, 0, 1)
    qa, ka, va, qb, kb, vb, kb32_s, vb32_s, ka32, va32 = _qkv(h, g, win, "sample")
    pos_q = past + np.arange(t_len)
    n_keys_a = -(-(l_a + t_len) // 128) * 128
    band_bias_s = _band_bias_sample(rel_bias[l], past, l_a, t_len, n_keys_a)
    ya, ak_s, av_s = _band_sample(qa, ka, va, ka32, va32, cache_a_k[l].reshape(n_s, l_a, A_WIDTH),
                                  cache_a_v[l].reshape(n_s, l_a, A_WIDTH), band_bias_s, t_len)
    bias_c = _alibi(pos_q, np.arange(past))
    pos_n = np.concatenate([pos_q, np.zeros(SAMPLE_KEY_PAD - t_len, np.int64)])
    bias_n = _alibi(pos_q, pos_n, invalid=np.broadcast_to(np.arange(SAMPLE_KEY_PAD) >= t_len,
                                                          (t_len, SAMPLE_KEY_PAD)))
    cache_kt = cache_b_k[l].transpose(0, 2, 3, 4, 1).reshape(n_s, B_WIDTH, past)
    cache_v = cache_b_v[l].reshape(n_s, past * B_HEADS, B_VDIM)
    yb = _diff_sample(qb, kb, vb, cache_kt, cache_v, bias_c, bias_n, lq, sg, lam_init, t_len)
    mk_s = cache_mem_k[l].reshape(n_s, n_mem, d).astype(BF16)
    mv_s = cache_mem_v[l].reshape(n_s, n_mem, d).astype(BF16)
    h = _merge(h, ya, yb, mk_s, mv_s, g, wg, bg, wa, wb, wo, wq, wmo, t_len)
    y_s = _ffn(h, g, up2, down2, 7, 8).reshape(n_s, t_len, d)

    dh_m = d // M_HEADS
    return (y_p, y_s,
            ak_p.reshape(1, n_p, keep_a, A_HEADS, A_HEAD_DIM), av_p.reshape(1, n_p, keep_a, A_HEADS, A_HEAD_DIM),
            kb32.reshape(1, n_p, s_len, B_HEADS, 2, B_HEAD_DIM), vb32.reshape(1, n_p, s_len, B_HEADS, B_VDIM),
            mkv32[:, :d].reshape(1, n_p, n_mem, M_HEADS, dh_m), mkv32[:, d:].reshape(1, n_p, n_mem, M_HEADS, dh_m),
            ak_s.reshape(1, n_s, l_a, A_HEADS, A_HEAD_DIM), av_s.reshape(1, n_s, l_a, A_HEADS, A_HEAD_DIM),
            kb32_s.reshape(1, n_s, t_len, B_HEADS, 2, B_HEAD_DIM), vb32_s.reshape(1, n_s, t_len, B_HEADS, B_VDIM))
```
